```python
import math
import jax, jax.numpy as jnp
from jax import lax
import numpy as np

D_MODEL = 1024
BATCH = 4
SEQ = 8192
DEPTH = 2

GRID_W = 64
CTX_LEN = 256
NORM_EPS = 1e-6
Q_BLOCK = 128
ROPE_THETA = 10000.0

MLA_HEADS = 6
MLA_NOPE_DIM = 64
MLA_ROPE_DIM = 32
MLA_QK_DIM = MLA_NOPE_DIM + MLA_ROPE_DIM
MLA_V_DIM = 64
MLA_Q_RANK = 384
MLA_KV_RANK = 256
MLA_WIDTH = MLA_HEADS * MLA_V_DIM

SWA_HEADS = 6
SWA_KV_HEADS = 2
SWA_GROUP = SWA_HEADS // SWA_KV_HEADS
SWA_HEAD_DIM = 64
SWA_WINDOW = 128
SWA_BLOCK = 128
SWA_BAND = SWA_BLOCK + 2 * SWA_WINDOW
SWA_WIDTH = SWA_HEADS * SWA_HEAD_DIM

HY_WIDTH = 256
HY_SHORT = 3
HY_BANDS = 8
HY_EMB = 1 + 2 * HY_BANDS
HY_HIDDEN = 64
HY_DECAY_FAST = math.log(1e-2) / 0.3
HY_DECAY_SLOW = math.log(1e-2) / 1.5

D_MIX = MLA_WIDTH + SWA_WIDTH + HY_WIDTH
IN_SIZES = (MLA_Q_RANK, MLA_KV_RANK, MLA_ROPE_DIM, MLA_WIDTH,
            SWA_HEADS * SWA_HEAD_DIM, SWA_KV_HEADS * SWA_HEAD_DIM, SWA_KV_HEADS * SWA_HEAD_DIM, SWA_WIDTH,
            3 * HY_WIDTH, HY_WIDTH)
IN_COLS = sum(IN_SIZES)
IN_SPLITS = tuple(int(s) for s in np.cumsum(IN_SIZES)[:-1])

kernel_name = "hybrid_mla_swa_hyena_dit"


def rmsnorm(x, g):
    xf = x.astype(jnp.float32)
    y = xf * lax.rsqrt(jnp.mean(xf * xf, axis=-1, keepdims=True) + NORM_EPS)
    return (y * g.astype(jnp.float32)).astype(x.dtype)


def axial_rope(num_tokens, rot_dim):
    rows = num_tokens // GRID_W
    row = jnp.repeat(jnp.arange(rows, dtype=jnp.float32), GRID_W)
    col = jnp.tile(jnp.arange(GRID_W, dtype=jnp.float32), rows)
    n_freq = rot_dim // 4
    freqs = ROPE_THETA ** (-jnp.arange(n_freq, dtype=jnp.float32) / n_freq)
    ang = jnp.concatenate([row[:, None] * freqs, col[:, None] * freqs], axis=-1)
    return jnp.cos(ang), jnp.sin(ang)


def apply_rope(x, rope):
    cos, sin = rope
    cos = cos[:, None, :].astype(x.dtype)
    sin = sin[:, None, :].astype(x.dtype)
    x1, x2 = jnp.split(x, 2, axis=-1)
    return jnp.concatenate([x1 * cos - x2 * sin, x2 * cos + x1 * sin], axis=-1)


def modulate_project(x, mod, norm_g, w_in):
    shift, scale, gate = jnp.split(mod, 3, axis=-1)
    h = rmsnorm(x, norm_g) * (1.0 + scale) + shift
    return h @ w_in, gate


def dense_attention(q, k, v, scale):
    s = jnp.einsum("bqhd,bkhd->bhqk", q, k).astype(jnp.float32) * scale
    p = jax.nn.softmax(s, axis=-1).astype(v.dtype)
    return jnp.einsum("bhqk,bkhd->bqhd", p, v)


def blocked_attention(q, k, v, scale):
    B, L, H, dk = q.shape
    qb = q.reshape(B, L // Q_BLOCK, Q_BLOCK, H, dk).transpose(1, 0, 2, 3, 4)
    out = lax.map(lambda qi: dense_attention(qi, k, v, scale), qb)
    return out.transpose(1, 0, 2, 3, 4).reshape(B, L, H, v.shape[-1])


def mla_queries(q_lat, lp, rope):
    B, L, _ = q_lat.shape
    q = (rmsnorm(q_lat, lp["mla_q_norm"]) @ lp["mla_w_uq"]).reshape(B, L, MLA_HEADS, MLA_QK_DIM)
    q_nope, q_rope = jnp.split(q, [MLA_NOPE_DIM], axis=-1)
    if rope is not None:
        q_rope = apply_rope(q_rope, rope)
    return jnp.concatenate([q_nope, q_rope], axis=-1)


def mla_keys_values(kv_lat, k_r, lp, rope):
    B, L, _ = kv_lat.shape
    kv = (rmsnorm(kv_lat, lp["mla_kv_norm"]) @ lp["mla_w_ukv"]).reshape(B, L, MLA_HEADS, MLA_NOPE_DIM + MLA_V_DIM)
    k_nope, v = jnp.split(kv, [MLA_NOPE_DIM], axis=-1)
    k_rope = k_r[:, :, None, :]
    if rope is not None:
        k_rope = apply_rope(k_rope, rope)
    k_rope = jnp.broadcast_to(k_rope, (B, L, MLA_HEADS, MLA_ROPE_DIM))
    return jnp.concatenate([k_nope, k_rope], axis=-1), v


def swa_latent(q, k, v, k_ctx, v_ctx, sink):
    f32 = jnp.float32
    B, L = q.shape[:2]
    nb = L // SWA_BLOCK
    qb = q.reshape(B, nb, SWA_BLOCK, SWA_KV_HEADS, SWA_GROUP, SWA_HEAD_DIM)
    pad = ((0, 0), (SWA_WINDOW, SWA_WINDOW), (0, 0), (0, 0))
    kp, vp = jnp.pad(k, pad), jnp.pad(v, pad)
    idx = jnp.arange(nb)[:, None] * SWA_BLOCK + jnp.arange(SWA_BAND)[None, :]
    kb, vb = kp[:, idx], vp[:, idx]
    kpos = idx - SWA_WINDOW
    qpos = jnp.arange(nb)[:, None] * SWA_BLOCK + jnp.arange(SWA_BLOCK)[None, :]
    diff = kpos[:, None, :] - qpos[:, :, None]
    valid = (jnp.abs(diff) <= SWA_WINDOW) & (kpos[:, None, :] >= 0) & (kpos[:, None, :] < L)
    scale = SWA_HEAD_DIM ** -0.5
    s_loc = jnp.einsum("bnqgrd,bnkgd->bngrqk", qb, kb).astype(f32) * scale
    s_loc = jnp.where(valid[None, :, None, None], s_loc, -jnp.inf)
    s_ctx = jnp.einsum("bnqgrd,bcgd->bngrqc", qb, k_ctx).astype(f32) * scale
    sink_g = sink.astype(f32).reshape(SWA_KV_HEADS, SWA_GROUP)[None, None, :, :, None, None]
    m = jnp.maximum(jnp.maximum(s_loc.max(-1, keepdims=True), s_ctx.max(-1, keepdims=True)), sink_g)
    p_loc = jnp.exp(s_loc - m)
    p_ctx = jnp.exp(s_ctx - m)
    denom = p_loc.sum(-1, keepdims=True) + p_ctx.sum(-1, keepdims=True) + jnp.exp(sink_g - m)
    o = (jnp.einsum("bngrqk,bnkgd->bnqgrd", (p_loc / denom).astype(v.dtype), vb)
         + jnp.einsum("bngrqc,bcgd->bnqgrd", (p_ctx / denom).astype(v.dtype), v_ctx))
    return o.reshape(B, L, SWA_HEADS, SWA_HEAD_DIM)


def swa_context(q, k, v, sink):
    f32 = jnp.float32
    B, C = q.shape[:2]
    qg = q.reshape(B, C, SWA_KV_HEADS, SWA_GROUP, SWA_HEAD_DIM)
    s = jnp.einsum("bqgrd,bkgd->bgrqk", qg, k).astype(f32) * SWA_HEAD_DIM ** -0.5
    sink_col = jnp.broadcast_to(sink.astype(f32).reshape(1, SWA_KV_HEADS, SWA_GROUP, 1, 1), s.shape[:-1] + (1,))
    p = jax.nn.softmax(jnp.concatenate([s, sink_col], axis=-1), axis=-1)[..., :-1]
    o = jnp.einsum("bgrqk,bkgd->bqgrd", p.astype(v.dtype), v)
    return o.reshape(B, C, SWA_HEADS, SWA_HEAD_DIM)


def short_conv(u, w, b):
    up = jnp.pad(u, ((0, 0), (1, 1), (0, 0)))
    return up[:, :-2] * w[0] + up[:, 1:-1] * w[1] + up[:, 2:] * w[2] + b


def hyena_filter(num_tokens, lp):
    f32 = jnp.float32
    t = jnp.linspace(0.0, 1.0, num_tokens, dtype=f32)[:, None]
    w = (2.0 * math.pi / num_tokens) * jnp.arange(num_tokens, dtype=f32)[:, None]
    bands = jnp.linspace(1e-4, HY_BANDS - 1, HY_BANDS, dtype=f32)[None, :]
    z = jnp.concatenate([t, jnp.cos(bands * w), -jnp.sin(bands * w)], axis=-1)
    freq = lp["hy_freq"].astype(f32)
    h = jnp.sin(freq * (z @ lp["hy_w1"].astype(f32) + lp["hy_b1"].astype(f32)))
    h = jnp.sin(freq * (h @ lp["hy_w2"].astype(f32) + lp["hy_b2"].astype(f32)))
    h = h @ lp["hy_w3"].astype(f32) + lp["hy_b3"].astype(f32)
    deltas = jnp.abs(jnp.linspace(HY_DECAY_FAST, HY_DECAY_SLOW, HY_WIDTH, dtype=f32))
    decay = jnp.exp(-t * deltas)
    h = h.reshape(num_tokens, 2, HY_WIDTH) * decay[:, None, :]
    kern = jnp.concatenate([h[:, 0], jnp.zeros((1, HY_WIDTH), f32), h[:0:-1, 1]], axis=0)
    return kern / (jnp.sum(jnp.abs(kern), axis=0, keepdims=True) + NORM_EPS)


def fft_long_conv(z, kern):
    L = z.shape[1]
    zf = jnp.fft.rfft(z.astype(jnp.float32), n=2 * L, axis=1)
    kf = jnp.fft.rfft(kern, n=2 * L, axis=0)
    y = jnp.fft.irfft(zf * kf[None], n=2 * L, axis=1)[:, :L]
    return y.astype(z.dtype)


def hyena_mix(u, lp):
    uc = short_conv(u, lp["hy_conv_w"], lp["hy_conv_b"])
    x0, x1, v = jnp.split(uc, 3, axis=-1)
    kern = hyena_filter(u.shape[1], lp)
    z = v * x1
    return x0 * (fft_long_conv(z, kern) + lp["hy_bias"] * z)


def merge_branches(o_a, o_b, o_h, g_a, g_b, g_h, w_out):
    B, L = o_a.shape[:2]
    y = jnp.concatenate([o_a.reshape(B, L, MLA_WIDTH) * jax.nn.silu(g_a),
                         o_b.reshape(B, L, SWA_WIDTH) * jax.nn.silu(g_b),
                         o_h * jax.nn.silu(g_h)], axis=-1)
    return y @ w_out


def hybrid_layer(x, xc, c, c_ctx, lp, update_ctx):
    B, L, _ = x.shape
    mod_x = (jax.nn.silu(c) @ lp["mod_w"] + lp["mod_b"])[:, None, :]
    mod_c = (jax.nn.silu(c_ctx) @ lp["mod_w"] + lp["mod_b"])[None, None, :]
    px, gate_x = modulate_project(x, mod_x, lp["norm_g"], lp["w_in"])
    pc, gate_c = modulate_project(xc, mod_c, lp["norm_g"], lp["w_in"])
    mq_x, mkv_x, mkr_x, mg_x, sq_x, sk_x, sv_x, sg_x, hu_x, hg_x = jnp.split(px, IN_SPLITS, axis=-1)
    mq_c, mkv_c, mkr_c, mg_c, sq_c, sk_c, sv_c, sg_c, hu_c, hg_c = jnp.split(pc, IN_SPLITS, axis=-1)
    C = xc.shape[1]
    rope_mla = axial_rope(L, MLA_ROPE_DIM)
    rope_swa = axial_rope(L, SWA_HEAD_DIM)

    k_a, v_a = mla_keys_values(mkv_x, mkr_x, lp, rope_mla)
    kc_a, vc_a = mla_keys_values(mkv_c, mkr_c, lp, None)
    q_a = mla_queries(mq_x, lp, rope_mla)
    o_a = blocked_attention(q_a, jnp.concatenate([k_a, kc_a], axis=1),
                            jnp.concatenate([v_a, vc_a], axis=1), MLA_QK_DIM ** -0.5)

    q_b = apply_rope(sq_x.reshape(B, L, SWA_HEADS, SWA_HEAD_DIM), rope_swa)
    k_b = apply_rope(sk_x.reshape(B, L, SWA_KV_HEADS, SWA_HEAD_DIM), rope_swa)
    v_b = sv_x.reshape(B, L, SWA_KV_HEADS, SWA_HEAD_DIM)
    kc_b = sk_c.reshape(B, C, SWA_KV_HEADS, SWA_HEAD_DIM)
    vc_b = sv_c.reshape(B, C, SWA_KV_HEADS, SWA_HEAD_DIM)
    o_b = swa_latent(q_b, k_b, v_b, kc_b, vc_b, lp["swa_sink"])

    o_h = hyena_mix(hu_x, lp)

    x_new = x + gate_x * merge_branches(o_a, o_b, o_h, mg_x, sg_x, hg_x, lp["w_out"])

    if update_ctx:
        oc_a = dense_attention(mla_queries(mq_c, lp, None), kc_a, vc_a, MLA_QK_DIM ** -0.5)
        oc_b = swa_context(sq_c.reshape(B, C, SWA_HEADS, SWA_HEAD_DIM), kc_b, vc_b, lp["swa_sink"])
        oc_h = hyena_mix(hu_c, lp)
        xc = xc + gate_c * merge_branches(oc_a, oc_b, oc_h, mg_c, sg_c, hg_c, lp["w_out"])
    return x_new, xc


def setup_inputs(seed: int = 0) -> dict:
    key = jax.random.key(seed)
    ks = jax.random.split(key, 32)
    f32 = jnp.float32

    def nrm(k, shape, scale):
        return jax.random.normal(k, shape, f32) * scale

    D = D_MODEL
    return {
        "x": nrm(ks[0], (BATCH, SEQ, D), 1.0),
        "c": nrm(ks[1], (BATCH, D), 1.0),
        "ctx": nrm(ks[2], (BATCH, CTX_LEN, D), 1.0),
        "c_ctx": nrm(ks[3], (D,), 1.0),
        "norm_g": 1.0 + nrm(ks[4], (DEPTH, D), 0.05),
        "mod_w": nrm(ks[5], (DEPTH, D, 3 * D), 0.5 * D ** -0.5),
        "mod_b": nrm(ks[6], (DEPTH, 3 * D), 0.02),
        "w_in": nrm(ks[7], (DEPTH, D, IN_COLS), D ** -0.5),
        "mla_q_norm": 1.0 + nrm(ks[8], (DEPTH, MLA_Q_RANK), 0.05),
        "mla_w_uq": nrm(ks[9], (DEPTH, MLA_Q_RANK, MLA_HEADS * MLA_QK_DIM), MLA_Q_RANK ** -0.5),
        "mla_kv_norm": 1.0 + nrm(ks[10], (DEPTH, MLA_KV_RANK), 0.05),
        "mla_w_ukv": nrm(ks[11], (DEPTH, MLA_KV_RANK, MLA_HEADS * (MLA_NOPE_DIM + MLA_V_DIM)), MLA_KV_RANK ** -0.5),
        "swa_sink": nrm(ks[12], (DEPTH, SWA_HEADS), 0.5),
        "hy_conv_w": nrm(ks[13], (DEPTH, HY_SHORT, 3 * HY_WIDTH), HY_SHORT ** -0.5),
        "hy_conv_b": nrm(ks[14], (DEPTH, 3 * HY_WIDTH), 0.02),
        "hy_w1": nrm(ks[15], (DEPTH, HY_EMB, HY_HIDDEN), HY_EMB ** -0.5),
        "hy_b1": nrm(ks[16], (DEPTH, HY_HIDDEN), 0.02),
        "hy_freq": 1.0 + nrm(ks[17], (DEPTH, HY_HIDDEN), 0.05),
        "hy_w2": nrm(ks[18], (DEPTH, HY_HIDDEN, HY_HIDDEN), HY_HIDDEN ** -0.5),
        "hy_b2": nrm(ks[19], (DEPTH, HY_HIDDEN), 0.02),
        "hy_w3": nrm(ks[20], (DEPTH, HY_HIDDEN, 2 * HY_WIDTH), HY_HIDDEN ** -0.5),
        "hy_b3": nrm(ks[21], (DEPTH, 2 * HY_WIDTH), 0.02),
        "hy_bias": nrm(ks[22], (DEPTH, HY_WIDTH), 0.5),
        "w_out": nrm(ks[23], (DEPTH, D_MIX, D), D_MIX ** -0.5),
        "final_norm_g": 1.0 + nrm(ks[24], (D,), 0.05),
    }


def reference(x, c, ctx, c_ctx, norm_g, mod_w, mod_b, w_in, mla_q_norm, mla_w_uq, mla_kv_norm, mla_w_ukv,
              swa_sink, hy_conv_w, hy_conv_b, hy_w1, hy_b1, hy_freq, hy_w2, hy_b2, hy_w3, hy_b3, hy_bias,
              w_out, final_norm_g):
    xc = ctx
    for l in range(DEPTH):
        lp = {
            "norm_g": norm_g[l], "mod_w": mod_w[l], "mod_b": mod_b[l], "w_in": w_in[l],
            "mla_q_norm": mla_q_norm[l], "mla_w_uq": mla_w_uq[l],
            "mla_kv_norm": mla_kv_norm[l], "mla_w_ukv": mla_w_ukv[l],
            "swa_sink": swa_sink[l],
            "hy_conv_w": hy_conv_w[l], "hy_conv_b": hy_conv_b[l],
            "hy_w1": hy_w1[l], "hy_b1": hy_b1[l], "hy_freq": hy_freq[l],
            "hy_w2": hy_w2[l], "hy_b2": hy_b2[l], "hy_w3": hy_w3[l], "hy_b3": hy_b3[l],
            "hy_bias": hy_bias[l], "w_out": w_out[l],
        }
        x, xc = hybrid_layer(x, xc, c, c_ctx, lp, update_ctx=(l < DEPTH - 1))
    return rmsnorm(x, final_norm_g)
```

```python
import functools
import math

import numpy as np
import jax
import jax.numpy as jnp
from jax import lax
from jax.experimental import pallas as pl
from jax.experimental.pallas import tpu as pltpu

F32 = jnp.float32
BF16 = jnp.bfloat16
HI = lax.Precision.HIGHEST

D_MODEL = 1024
DEPTH = 2
GRID_W = 64
NORM_EPS = 1e-6
ROPE_THETA = 10000.0

MLA_HEADS = 6
MLA_NOPE = 64
MLA_ROPE = 32
MLA_QK = MLA_NOPE + MLA_ROPE
MLA_V = 64
MLA_Q_RANK = 384
MLA_KV_RANK = 256
MLA_WIDTH = MLA_HEADS * MLA_V

SWA_HEADS = 6
SWA_KV_HEADS = 2
SWA_GROUP = SWA_HEADS // SWA_KV_HEADS
SWA_DIM = 64
SWA_WINDOW = 128
SWA_WIDTH = SWA_HEADS * SWA_DIM

HY_WIDTH = 256
HY_BANDS = 8
HY_EMB = 1 + 2 * HY_BANDS
HY_HIDDEN = 64
HY_DECAY_FAST = math.log(1e-2) / 0.3
HY_DECAY_SLOW = math.log(1e-2) / 1.5

D_MIX = MLA_WIDTH + SWA_WIDTH + HY_WIDTH
IN_SIZES = (MLA_Q_RANK, MLA_KV_RANK, MLA_ROPE, MLA_WIDTH, SWA_WIDTH, SWA_KV_HEADS * SWA_DIM,
            SWA_KV_HEADS * SWA_DIM, SWA_WIDTH, 3 * HY_WIDTH, HY_WIDTH)
IN_OFFS = tuple(int(v) for v in np.concatenate([[0], np.cumsum(IN_SIZES)]))

LANES = 128
FFT_N2 = 128

SEG_SIZES = (("mq", 384), ("mkv", 256), ("kr", 256), ("mg", 384), ("sq", 768), ("sk", 512),
             ("sv", 256), ("sg", 384), ("hu", 768), ("hg", 256))
SEG = {}
_o = 0
for _n, _s in SEG_SIZES:
    SEG[_n] = (_o, _o + _s)
    _o += _s
AUG_COLS = _o


def _silu(v):
    return v * jax.nn.sigmoid(v)


def _rms(v, g):
    return v * lax.rsqrt(jnp.mean(v * v, axis=-1, keepdims=True) + NORM_EPS) * g


def _rot_partner(w, half):
    return jnp.concatenate([-w[:, half:], w[:, :half]], axis=1)


def _prep_w_in(w_in):
    o = IN_OFFS
    mq, mkv, mkr, mg = (w_in[:, o[i]:o[i + 1]] for i in range(4))
    sq, sk, sv, sg, hu, hg = (w_in[:, o[i]:o[i + 1]] for i in range(4, 10))
    z = lambda n: jnp.zeros((D_MODEL, n), w_in.dtype)
    kr = jnp.concatenate([z(MLA_NOPE), mkr, z(32), z(MLA_NOPE), _rot_partner(mkr, MLA_ROPE // 2), z(32)], axis=1)
    sq_p = jnp.concatenate([_rot_partner(sq[:, SWA_DIM * h:SWA_DIM * (h + 1)], SWA_DIM // 2)
                            for h in range(SWA_HEADS)], axis=1)
    kh = [sk[:, SWA_DIM * g:SWA_DIM * (g + 1)] for g in range(SWA_KV_HEADS)]
    sk2 = jnp.concatenate([kh[0], kh[0], kh[1], kh[1]], axis=1)
    khp = [_rot_partner(k, SWA_DIM // 2) for k in kh]
    sk2_p = jnp.concatenate([khp[0], khp[0], khp[1], khp[1]], axis=1)
    vh = [sv[:, SWA_DIM * g:SWA_DIM * (g + 1)] for g in range(SWA_KV_HEADS)]
    sv2 = jnp.concatenate([vh[0], vh[0], vh[1], vh[1]], axis=1)
    w = jnp.concatenate([mq, mkv, kr, mg, sq, sq_p, sk2, sk2_p, sv2, sg, hu, hg], axis=1)
    assert w.shape[1] == AUG_COLS
    return w.astype(BF16)


def _prep_w_uq(w_uq):
    zr = lambda n: jnp.zeros((MLA_Q_RANK, n), w_uq.dtype)
    main, part = [], []
    for h in range(MLA_HEADS):
        nope = w_uq[:, MLA_QK * h:MLA_QK * h + MLA_NOPE]
        rope = w_uq[:, MLA_QK * h + MLA_NOPE:MLA_QK * (h + 1)]
        main += [nope, rope, zr(32)]
        part += [zr(MLA_NOPE), _rot_partner(rope, MLA_ROPE // 2), zr(32)]
    return jnp.concatenate(main + part, axis=1).astype(BF16)


def _prep_w_ukv(w_ukv):
    zr = lambda n: jnp.zeros((MLA_KV_RANK, n), w_ukv.dtype)
    ks, vs = [], []
    for h in range(MLA_HEADS):
        ks += [w_ukv[:, 128 * h:128 * h + MLA_NOPE], zr(64)]
        vs += [w_ukv[:, 128 * h + MLA_NOPE:128 * (h + 1)]]
    return jnp.concatenate(ks + vs, axis=1).astype(BF16)


def _rope_tables(num_tokens, rot_dim):
    rows = num_tokens // GRID_W
    row = jnp.repeat(jnp.arange(rows, dtype=F32), GRID_W)
    col = jnp.tile(jnp.arange(GRID_W, dtype=F32), rows)
    n_freq = rot_dim // 4
    freqs = ROPE_THETA ** (-jnp.arange(n_freq, dtype=F32) / n_freq)
    ang = jnp.concatenate([row[:, None] * freqs, col[:, None] * freqs], axis=-1)
    return jnp.cos(ang), jnp.sin(ang)


def _mla_rope_lanes(cos, sin):
    n = cos.shape[0]
    one, zero = jnp.ones((n, MLA_NOPE), F32), jnp.zeros((n, 32), F32)
    cm = jnp.concatenate([one, cos, cos, zero], axis=1)
    sm = jnp.concatenate([jnp.zeros((n, MLA_NOPE), F32), sin, sin, zero], axis=1)
    return cm, sm


def _swa_rope_lanes(cos, sin):
    return jnp.concatenate([cos] * 4, axis=1), jnp.concatenate([sin] * 4, axis=1)


def _mod_kernel(c_ref, w_ref, b_ref, o_ref):
    s = _silu(c_ref[...])
    o_ref[...] = jnp.dot(s, w_ref[...], precision=HI, preferred_element_type=F32) + b_ref[...]


def _mod_call(cc, mod_w, mod_b):
    rows, n = cc.shape[0], mod_w.shape[1]
    tn = 512
    return pl.pallas_call(
        _mod_kernel,
        grid=(n // tn,),
        in_specs=[pl.BlockSpec((rows, D_MODEL), lambda j: (0, 0)),
                  pl.BlockSpec((D_MODEL, tn), lambda j: (0, j)),
                  pl.BlockSpec((1, tn), lambda j: (0, j))],
        out_specs=pl.BlockSpec((rows, tn), lambda j: (0, j)),
        out_shape=jax.ShapeDtypeStruct((rows, n), F32),
        name="mod",
    )(cc, mod_w, mod_b.reshape(1, n))


def _proj_kernel(x_ref, shift_ref, scale_ref, g_ref, w_ref, gq_ref, wuq_ref, gkv_ref, wukv_ref,
                 cm_ref, sm_ref, cs_ref, ss_ref,
                 qm_ref, km_ref, vm_ref, qs_ref, ks_ref, vs_ref, gate_ref, hu_ref):
    x = x_ref[0]
    y = _rms(x, g_ref[...])
    h = (y * (1.0 + scale_ref[0]) + shift_ref[0]).astype(BF16)

    def seg(name):
        a, b = SEG[name]
        return jnp.dot(h, w_ref[:, a:b], preferred_element_type=F32)

    cm, sm = cm_ref[...], sm_ref[...]
    cs, ss = cs_ref[...], ss_ref[...]

    qn = _rms(seg("mq"), gq_ref[...]).astype(BF16)
    q2 = jnp.dot(qn, wuq_ref[...], preferred_element_type=F32)
    qscale = MLA_QK ** -0.5
    half = MLA_HEADS * LANES
    for hd in range(MLA_HEADS):
        a = q2[:, LANES * hd:LANES * (hd + 1)]
        b = q2[:, half + LANES * hd:half + LANES * (hd + 1)]
        qm_ref[0, :, LANES * hd:LANES * (hd + 1)] = ((a * cm + b * sm) * qscale).astype(BF16)

    kvn = _rms(seg("mkv"), gkv_ref[...]).astype(BF16)
    kv = jnp.dot(kvn, wukv_ref[...], preferred_element_type=F32)
    kr = seg("kr")
    kr = kr[:, :LANES] * cm + kr[:, LANES:] * sm
    for hd in range(MLA_HEADS):
        km_ref[0, :, LANES * hd:LANES * (hd + 1)] = (kv[:, LANES * hd:LANES * (hd + 1)] + kr).astype(BF16)
    vm_ref[0] = kv[:, half:].astype(BF16)

    sq = seg("sq")
    for j in range(3):
        a = sq[:, LANES * j:LANES * (j + 1)]
        b = sq[:, 384 + LANES * j:384 + LANES * (j + 1)]
        qs_ref[0, :, LANES * j:LANES * (j + 1)] = ((a * cs + b * ss) * (SWA_DIM ** -0.5)).astype(BF16)
    sk = seg("sk")
    for j in range(2):
        a = sk[:, LANES * j:LANES * (j + 1)]
        b = sk[:, 256 + LANES * j:256 + LANES * (j + 1)]
        ks_ref[0, :, LANES * j:LANES * (j + 1)] = (a * cs + b * ss).astype(BF16)
    vs_ref[0] = seg("sv").astype(BF16)

    gate_ref[0, :, 0:384] = _silu(seg("mg")).astype(BF16)
    gate_ref[0, :, 384:768] = _silu(seg("sg")).astype(BF16)
    gate_ref[0, :, 768:1024] = _silu(seg("hg")).astype(BF16)
    hu_ref[0] = seg("hu")


def _proj_call(x, shift, scale, norm_g, w_aug, gq, wuq, gkv, wukv, cm, sm, cs, ss, tm):
    nb, n, _ = x.shape
    row = lambda b, i: (b, i, 0)
    per_b = lambda b, i: (b, 0, 0)
    const = lambda b, i: (0, 0)
    tab = lambda b, i: (i, 0)
    widths = (768, 768, 384, 384, 256, 256, 1024, 768)
    dtypes = (BF16,) * 7 + (F32,)
    return pl.pallas_call(
        _proj_kernel,
        grid=(nb, n // tm),
        in_specs=[pl.BlockSpec((1, tm, D_MODEL), row),
                  pl.BlockSpec((1, 1, D_MODEL), per_b), pl.BlockSpec((1, 1, D_MODEL), per_b),
                  pl.BlockSpec((1, D_MODEL), const),
                  pl.BlockSpec((D_MODEL, AUG_COLS), const),
                  pl.BlockSpec((1, MLA_Q_RANK), const), pl.BlockSpec(wuq.shape, const),
                  pl.BlockSpec((1, MLA_KV_RANK), const), pl.BlockSpec(wukv.shape, const),
                  pl.BlockSpec((tm, LANES), tab), pl.BlockSpec((tm, LANES), tab),
                  pl.BlockSpec((tm, LANES), tab), pl.BlockSpec((tm, LANES), tab)],
        out_specs=[pl.BlockSpec((1, tm, w), row) for w in widths],
        out_shape=[jax.ShapeDtypeStruct((nb, n, w), d) for w, d in zip(widths, dtypes)],
        compiler_params=pltpu.CompilerParams(dimension_semantics=("parallel", "parallel")),
        name="proj",
    )(x, shift, scale, norm_g, w_aug, gq, wuq, gkv, wukv, cm, sm, cs, ss)


def _mla_kernel(*refs, n_lat, tk, has_lat):
    if has_lat:
        q_ref, kl_ref, vl_ref, kc_ref, vc_ref, o_ref, m_sc, l_sc, acc_sc = refs
    else:
        q_ref, kc_ref, vc_ref, o_ref, m_sc, l_sc, acc_sc = refs
    tq = q_ref.shape[1]
    nt = (((1,), (1,)), ((), ()))
    outs = []
    for hh in range(2):
        q = q_ref[0, :, LANES * hh:LANES * (hh + 1)]
        m_sc[...] = jnp.full(m_sc.shape, -jnp.inf, F32)
        l_sc[...] = jnp.zeros(l_sc.shape, F32)
        acc_sc[...] = jnp.zeros(acc_sc.shape, F32)

        def step(k, v):
            s = lax.dot_general(q, k, nt, preferred_element_type=F32)
            m_prev = m_sc[...]
            m_new = jnp.maximum(m_prev, jnp.max(s, axis=1, keepdims=True))
            alpha = jnp.exp(m_prev - m_new)
            p = jnp.exp(s - jnp.tile(m_new, (1, k.shape[0] // LANES)))
            l_sc[...] = alpha * l_sc[...] + jnp.sum(p, axis=1, keepdims=True)
            acc_sc[...] = alpha * acc_sc[...] + jnp.dot(p.astype(BF16), v, preferred_element_type=F32)
            m_sc[...] = m_new

        if has_lat:
            def body(i, carry):
                off = pl.multiple_of(i * tk, tk)
                step(kl_ref[0, pl.ds(off, tk), LANES * hh:LANES * (hh + 1)], vl_ref[0, pl.ds(off, tk), :])
                return carry
            lax.fori_loop(0, n_lat // tk, body, 0)
        step(kc_ref[0, :, LANES * hh:LANES * (hh + 1)], vc_ref[0])
        outs.append(acc_sc[...] / l_sc[...])
    lane = lax.broadcasted_iota(jnp.int32, (tq, LANES), 1)
    o_ref[0] = jnp.where(lane < MLA_V, outs[0], outs[1])


def _mla_call(q, k_lat, v_lat, k_ctx, v_ctx, tq, tk):
    nb, nq, _ = q.shape
    nc = k_ctx.shape[1]
    has_lat = k_lat is not None
    qmap = lambda b, p, i: (b, i, p)
    kvmap = lambda b, p, i: (b, 0, p)
    in_specs = [pl.BlockSpec((1, tq, 2 * LANES), qmap)]
    args = [q]
    n_lat = 0
    if has_lat:
        n_lat = k_lat.shape[1]
        in_specs += [pl.BlockSpec((1, n_lat, 2 * LANES), kvmap), pl.BlockSpec((1, n_lat, LANES), kvmap)]
        args += [k_lat, v_lat]
    in_specs += [pl.BlockSpec((1, nc, 2 * LANES), kvmap), pl.BlockSpec((1, nc, LANES), kvmap)]
    args += [k_ctx, v_ctx]
    return pl.pallas_call(
        functools.partial(_mla_kernel, n_lat=n_lat, tk=tk, has_lat=has_lat),
        grid=(nb, MLA_HEADS // 2, nq // tq),
        in_specs=in_specs,
        out_specs=pl.BlockSpec((1, tq, LANES), qmap),
        out_shape=jax.ShapeDtypeStruct((nb, nq, MLA_WIDTH), F32),
        scratch_shapes=[pltpu.VMEM((tq, LANES), F32), pltpu.VMEM((tq, LANES), F32), pltpu.VMEM((tq, LANES), F32)],
        compiler_params=pltpu.CompilerParams(dimension_semantics=("parallel", "parallel", "parallel")),
        name="mla_lat" if has_lat else "mla_ctx",
    )(*args)


def _swa_kernel(*refs, seq, band):
    if band:
        sink_ref, q_ref, kp_ref, kc_ref, kn_ref, vp_ref, vc_ref, vn_ref, kx_ref, vx_ref, o_ref = refs
    else:
        sink_ref, q_ref, kx_ref, vx_ref, o_ref = refs
    tq = q_ref.shape[1]
    nt = (((1,), (1,)), ((), ()))
    lane = lax.broadcasted_iota(jnp.int32, (tq, LANES), 1)
    lo = lane < SWA_DIM
    if band:
        i = pl.program_id(1)
        kband = jnp.concatenate([kp_ref[0], kc_ref[0], kn_ref[0]], axis=0)
        vband = jnp.concatenate([vp_ref[0], vc_ref[0], vn_ref[0]], axis=0)
        qpos = i * tq + lax.broadcasted_iota(jnp.int32, (tq, 3 * tq), 0)
        kpos = (i - 1) * tq + lax.broadcasted_iota(jnp.int32, (tq, 3 * tq), 1)
        valid = (jnp.abs(kpos - qpos) <= SWA_WINDOW) & (kpos >= 0) & (kpos < seq)
    kx, vx = kx_ref[0], vx_ref[0]
    for pr in range(SWA_HEADS // 2):
        qp = q_ref[0, :, LANES * pr:LANES * (pr + 1)]
        res = []
        for hh in range(2):
            hd = 2 * pr + hh
            g = hd // SWA_GROUP
            q = jnp.where(lo if hh == 0 else jnp.logical_not(lo), qp, jnp.zeros_like(qp))
            sink = sink_ref[hd]
            s_ctx = lax.dot_general(q, kx[:, LANES * g:LANES * (g + 1)], nt, preferred_element_type=F32)
            m = jnp.maximum(jnp.max(s_ctx, axis=1, keepdims=True), sink)
            if band:
                s_loc = lax.dot_general(q, kband[:, LANES * g:LANES * (g + 1)], nt, preferred_element_type=F32)
                s_loc = jnp.where(valid, s_loc, -jnp.inf)
                m = jnp.maximum(m, jnp.max(s_loc, axis=1, keepdims=True))
            p_ctx = jnp.exp(s_ctx - m)
            denom = jnp.sum(p_ctx, axis=1, keepdims=True) + jnp.exp(sink - m)
            o = jnp.dot(p_ctx.astype(BF16), vx[:, LANES * g:LANES * (g + 1)], preferred_element_type=F32)
            if band:
                p_loc = jnp.exp(s_loc - m)
                denom = denom + jnp.sum(p_loc, axis=1, keepdims=True)
                o = o + jnp.dot(p_loc.astype(BF16), vband[:, LANES * g:LANES * (g + 1)], preferred_element_type=F32)
            res.append(o / denom)
        o_ref[0, :, LANES * pr:LANES * (pr + 1)] = jnp.where(lo, res[0], res[1])


def _swa_call(sink, q, k, v, k_ctx, v_ctx, band):
    nb, nq, _ = q.shape
    nc = k_ctx.shape[1]
    tq = SWA_WINDOW if band else nq
    nblk = nq // tq
    cur = lambda b, i: (b, i, 0)
    prev = lambda b, i: (b, jnp.maximum(i - 1, 0), 0)
    nxt = lambda b, i: (b, jnp.minimum(i + 1, nblk - 1), 0)
    ctx = lambda b, i: (b, 0, 0)
    kw = 2 * LANES
    in_specs = [pl.BlockSpec(memory_space=pltpu.SMEM), pl.BlockSpec((1, tq, SWA_WIDTH), cur)]
    args = [sink, q]
    if band:
        in_specs += [pl.BlockSpec((1, tq, kw), prev), pl.BlockSpec((1, tq, kw), cur), pl.BlockSpec((1, tq, kw), nxt)] * 2
        args += [k, k, k, v, v, v]
    in_specs += [pl.BlockSpec((1, nc, kw), ctx), pl.BlockSpec((1, nc, kw), ctx)]
    args += [k_ctx, v_ctx]
    return pl.pallas_call(
        functools.partial(_swa_kernel, seq=nq, band=band),
        grid=(nb, nblk),
        in_specs=in_specs,
        out_specs=pl.BlockSpec((1, tq, SWA_WIDTH), cur),
        out_shape=jax.ShapeDtypeStruct((nb, nq, SWA_WIDTH), F32),
        compiler_params=pltpu.CompilerParams(dimension_semantics=("parallel", "parallel")),
        name="swa_lat" if band else "swa_ctx",
    )(*args)


def _filter_kernel(z_ref, w1_ref, b1_ref, fr_ref, w2_ref, b2_ref, w3_ref, b3_ref, dl_ref,
                   hf_ref, hb_ref, nrm_ref, *, num_tokens):
    i = pl.program_id(0)
    tl = z_ref.shape[0]
    fr = fr_ref[...]
    h = jnp.sin(fr * (jnp.dot(z_ref[...], w1_ref[...], precision=HI, preferred_element_type=F32) + b1_ref[...]))
    h = jnp.sin(fr * (jnp.dot(h, w2_ref[...], precision=HI, preferred_element_type=F32) + b2_ref[...]))
    h = jnp.dot(h, w3_ref[...], precision=HI, preferred_element_type=F32) + b3_ref[...]
    row = i * tl + lax.broadcasted_iota(jnp.int32, (tl, HY_WIDTH), 0)
    t = row.astype(F32) * (1.0 / (num_tokens - 1))
    decay = jnp.exp(-t * dl_ref[...])
    hf = h[:, :HY_WIDTH] * decay
    hb = jnp.where(row == 0, 0.0, h[:, HY_WIDTH:] * decay)
    hf_ref[...] = hf
    hb_ref[...] = hb
    part = jnp.sum(jnp.abs(hf) + jnp.abs(hb), axis=0, keepdims=True)

    @pl.when(i == 0)
    def _():
        nrm_ref[...] = jnp.zeros(nrm_ref.shape, F32)

    nrm_ref[...] += part


def _filter_call(lp, num_tokens):
    tl = min(num_tokens, 1024)
    t = jnp.linspace(0.0, 1.0, num_tokens, dtype=F32)[:, None]
    w = (2.0 * math.pi / num_tokens) * jnp.arange(num_tokens, dtype=F32)[:, None]
    bands = jnp.linspace(1e-4, HY_BANDS - 1, HY_BANDS, dtype=F32)[None, :]
    z = jnp.concatenate([t, jnp.cos(bands * w), -jnp.sin(bands * w),
                         jnp.zeros((num_tokens, LANES - HY_EMB), F32)], axis=-1)
    w1 = jnp.concatenate([lp["hy_w1"], jnp.zeros((LANES - HY_EMB, HY_HIDDEN), F32)], axis=0)
    deltas = jnp.abs(jnp.linspace(HY_DECAY_FAST, HY_DECAY_SLOW, HY_WIDTH, dtype=F32)).reshape(1, HY_WIDTH)
    const = lambda i: (0, 0)
    rowm = lambda i: (i, 0)
    r1 = lambda a: a.reshape(1, -1)
    return pl.pallas_call(
        functools.partial(_filter_kernel, num_tokens=num_tokens),
        grid=(num_tokens // tl,),
        in_specs=[pl.BlockSpec((tl, LANES), rowm),
                  pl.BlockSpec((LANES, HY_HIDDEN), const), pl.BlockSpec((1, HY_HIDDEN), const),
                  pl.BlockSpec((1, HY_HIDDEN), const),
                  pl.BlockSpec((HY_HIDDEN, HY_HIDDEN), const), pl.BlockSpec((1, HY_HIDDEN), const),
                  pl.BlockSpec((HY_HIDDEN, 2 * HY_WIDTH), const), pl.BlockSpec((1, 2 * HY_WIDTH), const),
                  pl.BlockSpec((1, HY_WIDTH), const)],
        out_specs=[pl.BlockSpec((tl, HY_WIDTH), rowm), pl.BlockSpec((tl, HY_WIDTH), rowm),
                   pl.BlockSpec((1, HY_WIDTH), const)],
        out_shape=[jax.ShapeDtypeStruct((num_tokens, HY_WIDTH), F32)] * 2 + [jax.ShapeDtypeStruct((1, HY_WIDTH), F32)],
        compiler_params=pltpu.CompilerParams(dimension_semantics=("arbitrary",)),
        name="hy_filter",
    )(z, w1, r1(lp["hy_b1"]), r1(lp["hy_freq"]), lp["hy_w2"], r1(lp["hy_b2"]), lp["hy_w3"], r1(lp["hy_b3"]), deltas)


def _short_conv(u, prev_row, next_row, w_ref, b_ref):
    n = u.shape[0]
    row = lax.broadcasted_iota(jnp.int32, u.shape, 0)
    up = jnp.where(row == 0, prev_row, pltpu.roll(u, 1, 0))
    un = jnp.where(row == n - 1, next_row, pltpu.roll(u, n - 1, 0))
    return up * w_ref[0:1, :] + u * w_ref[1:2, :] + un * w_ref[2:3, :] + b_ref[...]


def _hconv_kernel(u_ref, up_ref, un_ref, w_ref, b_ref, x0_ref, z_ref):
    i = pl.program_id(1)
    last = pl.num_programs(1) - 1
    prev_row = jnp.where(i == 0, 0.0, up_ref[0, 7:8, :])
    next_row = jnp.where(i == last, 0.0, un_ref[0, 0:1, :])
    uc = _short_conv(u_ref[0], prev_row, next_row, w_ref, b_ref)
    x0_ref[0] = uc[:, :HY_WIDTH]
    z_ref[0] = uc[:, 2 * HY_WIDTH:] * uc[:, HY_WIDTH:2 * HY_WIDTH]


def _hconv_call(hu, conv_w, conv_b, tl):
    nb, n, w3 = hu.shape
    hb = tl // 8
    nh = n // 8
    cur = lambda b, i: (b, i, 0)
    prev = lambda b, i: (b, jnp.maximum(i * hb - 1, 0), 0)
    nxt = lambda b, i: (b, jnp.minimum((i + 1) * hb, nh - 1), 0)
    const = lambda b, i: (0, 0)
    return pl.pallas_call(
        _hconv_kernel,
        grid=(nb, n // tl),
        in_specs=[pl.BlockSpec((1, tl, w3), cur), pl.BlockSpec((1, 8, w3), prev), pl.BlockSpec((1, 8, w3), nxt),
                  pl.BlockSpec((3, w3), const), pl.BlockSpec((1, w3), const)],
        out_specs=[pl.BlockSpec((1, tl, HY_WIDTH), cur)] * 2,
        out_shape=[jax.ShapeDtypeStruct((nb, n, HY_WIDTH), F32)] * 2,
        compiler_params=pltpu.CompilerParams(dimension_semantics=("parallel", "parallel")),
        name="hy_conv",
    )(hu, hu, hu, conv_w, conv_b.reshape(1, w3))


def _dft_tables(seq):
    n1 = 2 * seq // FFT_N2
    nd = seq // FFT_N2
    n = n1 * FFT_N2
    k1 = np.arange(n1)[:, None]
    a = 2.0 * np.pi * ((k1 * np.arange(nd)[None, :]) % n1) / n1
    c, s = np.cos(a), np.sin(a)
    fwd1 = np.block([[c, s], [-s, c]])
    inv1 = np.block([[c.T, -s.T], [s.T, c.T]])
    k1j = jnp.arange(n1, dtype=jnp.int32)[:, None, None]
    k2 = jnp.arange(FFT_N2, dtype=jnp.int32)[None, :, None]
    n2 = jnp.arange(FFT_N2, dtype=jnp.int32)[None, None, :]
    ph = (n2 * (k1j + n1 * k2)) % n
    g = ph.astype(F32) * (2.0 * math.pi / n)
    gr, gi = jnp.cos(g), -jnp.sin(g)
    cast = lambda t: jnp.asarray(t, BF16)
    return (cast(fwd1), cast(inv1), cast(gr), cast(gi),
            cast(jnp.transpose(gr, (0, 2, 1))), cast(jnp.transpose(-gi, (0, 2, 1))))


def _stage1_kernel(m_ref, z_ref, a_ref):
    zs = jnp.concatenate([z_ref[0, 0], z_ref[0, 1]], axis=0).astype(BF16)
    a_ref[0] = jnp.dot(m_ref[...], zs, preferred_element_type=F32).astype(a_ref.dtype)


def _stage1_call(mat, z, out_dtype, tn, name):
    npair, _, rows, cols = z.shape
    mo = mat.shape[0]
    return pl.pallas_call(
        _stage1_kernel,
        grid=(npair, cols // tn),
        in_specs=[pl.BlockSpec(mat.shape, lambda p, j: (0, 0)),
                  pl.BlockSpec((1, 2, rows, tn), lambda p, j: (p, 0, 0, j))],
        out_specs=pl.BlockSpec((1, mo, tn), lambda p, j: (p, 0, j)),
        out_shape=jax.ShapeDtypeStruct((npair, mo, cols), out_dtype),
        compiler_params=pltpu.CompilerParams(dimension_semantics=("parallel", "parallel")),
        name=name,
    )(mat, z)


def _cplx_mat(re, im):
    return jnp.concatenate([jnp.concatenate([re, -im], axis=1), jnp.concatenate([im, re], axis=1)], axis=0)


def _spec_kernel(gr_ref, gi_ref, a_ref, nrm_ref, k_ref, *, kb, scale):
    inv = scale / (nrm_ref[...] + NORM_EPS)
    for j in range(kb):
        g2 = _cplx_mat(gr_ref[j], gi_ref[j])
        x1 = jnp.dot(g2, jnp.concatenate([a_ref[0, 0, j], a_ref[0, 1, j]], axis=0), preferred_element_type=F32)
        x2 = jnp.dot(g2, jnp.concatenate([a_ref[1, 0, j], a_ref[1, 1, j]], axis=0), preferred_element_type=F32)
        k_ref[j, :FFT_N2, :] = (x1[:FFT_N2] + x2[:FFT_N2]) * inv
        k_ref[j, FFT_N2:, :] = (x1[FFT_N2:] - x2[FFT_N2:]) * inv


def _spec_call(gr, gi, a, nrm, kb, scale):
    n1 = gr.shape[0]
    tab = lambda i: (i, 0, 0)
    return pl.pallas_call(
        functools.partial(_spec_kernel, kb=kb, scale=scale),
        grid=(n1 // kb,),
        in_specs=[pl.BlockSpec((kb, FFT_N2, FFT_N2), tab), pl.BlockSpec((kb, FFT_N2, FFT_N2), tab),
                  pl.BlockSpec((2, 2, kb, FFT_N2, HY_WIDTH), lambda i: (0, 0, i, 0, 0)),
                  pl.BlockSpec((1, HY_WIDTH), lambda i: (0, 0))],
        out_specs=pl.BlockSpec((kb, 2 * FFT_N2, HY_WIDTH), tab),
        out_shape=jax.ShapeDtypeStruct((n1, 2 * FFT_N2, HY_WIDTH), F32),
        compiler_params=pltpu.CompilerParams(dimension_semantics=("parallel",)),
        name="hy_spec",
    )(gr, gi, a, nrm)


def _stage2_kernel(gr_ref, gi_ref, hr_ref, hi_ref, k_ref, a_ref, b_ref, *, kb):
    for j in range(kb):
        s = jnp.concatenate([a_ref[0, 0, j], a_ref[0, 1, j]], axis=0)
        x = jnp.dot(_cplx_mat(gr_ref[j], gi_ref[j]), s, preferred_element_type=F32)
        xr, xi = x[:FFT_N2], x[FFT_N2:]
        kr, ki = k_ref[j, :FFT_N2, :], k_ref[j, FFT_N2:, :]
        y = jnp.concatenate([xr * kr - xi * ki, xr * ki + xi * kr], axis=0).astype(BF16)
        b = jnp.dot(_cplx_mat(hr_ref[j], hi_ref[j]), y, preferred_element_type=F32)
        b_ref[0, 0, j] = b[:FFT_N2].astype(b_ref.dtype)
        b_ref[0, 1, j] = b[FFT_N2:].astype(b_ref.dtype)


def _stage2_call(gr, gi, hr, hi, kf, a, kb):
    npair = a.shape[0]
    n1 = gr.shape[0]
    tab = lambda p, i: (i, 0, 0)
    dat = lambda p, i: (p, 0, i, 0, 0)
    return pl.pallas_call(
        functools.partial(_stage2_kernel, kb=kb),
        grid=(npair, n1 // kb),
        in_specs=[pl.BlockSpec((kb, FFT_N2, FFT_N2), tab)] * 4
                 + [pl.BlockSpec((kb, 2 * FFT_N2, HY_WIDTH), tab),
                    pl.BlockSpec((1, 2, kb, FFT_N2, HY_WIDTH), dat)],
        out_specs=pl.BlockSpec((1, 2, kb, FFT_N2, HY_WIDTH), dat),
        out_shape=jax.ShapeDtypeStruct(a.shape, BF16),
        compiler_params=pltpu.CompilerParams(dimension_semantics=("parallel", "parallel")),
        name="hy_stage2",
    )(gr, gi, hr, hi, kf, a)


def _long_conv(z, hf, hb, nrm, tables):
    nb, seq, w = z.shape
    fwd1, inv1, gr, gi, hr, hi = tables
    n1 = 2 * seq // FFT_N2
    nd = seq // FFT_N2
    cols = FFT_N2 * w
    tn = min(cols, 4096)
    kb = min(n1, 8)
    zero = jnp.zeros_like(hf)
    zf = jnp.stack([jnp.stack([hf, zero]), jnp.stack([hb, zero])]).reshape(2, 2, nd, cols)
    af = _stage1_call(fwd1, zf, BF16, tn, "hy_fstage1").reshape(2, 2, n1, FFT_N2, w)
    kf = _spec_call(gr, gi, af, nrm, kb, 1.0 / (n1 * FFT_N2))
    zd = z.reshape(nb // 2, 2, nd, cols)
    a = _stage1_call(fwd1, zd, BF16, tn, "hy_stage1").reshape(nb // 2, 2, n1, FFT_N2, w)
    b = _stage2_call(gr, gi, hr, hi, kf, a, kb).reshape(nb // 2, 2, n1, cols)
    y = _stage1_call(inv1, b, F32, tn, "hy_istage1")
    return y.reshape(nb, seq, w)


def _hctx_kernel(u_ref, w_ref, b_ref, hf_ref, hb_ref, nrm_ref, c_ref, s_ref, ct_ref, st_ref,
                 x0_ref, z_ref, y_ref):
    u = u_ref[0]
    n = u.shape[0]
    zero_row = jnp.zeros((1, u.shape[1]), F32)
    uc = _short_conv(u, zero_row, zero_row, w_ref, b_ref)
    x0 = uc[:, :HY_WIDTH]
    z = uc[:, 2 * HY_WIDTH:] * uc[:, HY_WIDTH:2 * HY_WIDTH]
    dot = lambda a, b: jnp.dot(a, b, precision=HI, preferred_element_type=F32)
    c, s = c_ref[...], s_ref[...]
    inv = (1.0 / (2 * n)) / (nrm_ref[...] + NORM_EPS)
    kr = (dot(c, hf_ref[...]) + dot(c, hb_ref[...])) * inv
    ki = (dot(s, hb_ref[...]) - dot(s, hf_ref[...])) * inv
    xr, xi = dot(c, z), -dot(s, z)
    yr, yi = xr * kr - xi * ki, xr * ki + xi * kr
    x0_ref[0] = x0
    z_ref[0] = z
    y_ref[0] = dot(ct_ref[...], yr) - dot(st_ref[...], yi)


def _hctx_call(hu, conv_w, conv_b, hf, hb, nrm):
    nb, n, w3 = hu.shape
    k = np.arange(2 * n)[:, None]
    a = 2.0 * np.pi * ((k * np.arange(n)[None, :]) % (2 * n)) / (2 * n)
    c, s = np.cos(a).astype(np.float32), np.sin(a).astype(np.float32)
    const = lambda b: (0, 0)
    cur = lambda b: (b, 0, 0)
    out = pl.BlockSpec((1, n, HY_WIDTH), cur)
    return pl.pallas_call(
        _hctx_kernel,
        grid=(nb,),
        in_specs=[pl.BlockSpec((1, n, w3), cur), pl.BlockSpec((3, w3), const), pl.BlockSpec((1, w3), const),
                  pl.BlockSpec((n, HY_WIDTH), const), pl.BlockSpec((n, HY_WIDTH), const),
                  pl.BlockSpec((1, HY_WIDTH), const),
                  pl.BlockSpec((2 * n, n), const), pl.BlockSpec((2 * n, n), const),
                  pl.BlockSpec((n, 2 * n), const), pl.BlockSpec((n, 2 * n), const)],
        out_specs=[out, out, out],
        out_shape=[jax.ShapeDtypeStruct((nb, n, HY_WIDTH), F32)] * 3,
        compiler_params=pltpu.CompilerParams(dimension_semantics=("parallel",)),
        name="hy_ctx",
    )(hu, conv_w, conv_b.reshape(1, w3), hf, hb, nrm, jnp.asarray(c), jnp.asarray(s),
      jnp.asarray(c.T.copy()), jnp.asarray(s.T.copy()))


def _merge_kernel(x_ref, oa_ref, ob_ref, x0_ref, z_ref, yc_ref, gate_ref, gx_ref, bias_ref, w_ref, fg_ref,
                  o_ref, *, final):
    g = gate_ref[0].astype(F32)
    ya = oa_ref[0] * g[:, 0:384]
    yb = ob_ref[0] * g[:, 384:768]
    z = z_ref[0]
    yh = x0_ref[0] * (yc_ref[0] + bias_ref[...] * z) * g[:, 768:1024]
    y = jnp.concatenate([ya, yb, yh], axis=1).astype(BF16)
    xn = x_ref[0] + gx_ref[0] * jnp.dot(y, w_ref[...], preferred_element_type=F32)
    if final:
        xn = _rms(xn, fg_ref[...])
    o_ref[0] = xn


def _merge_call(x, oa, ob, x0, z, yc, gates, gate_x, bias, w_out, fg, tm, final):
    nb, n, _ = x.shape
    row = lambda b, i: (b, i, 0)
    per_b = lambda b, i: (b, 0, 0)
    const = lambda b, i: (0, 0)
    return pl.pallas_call(
        functools.partial(_merge_kernel, final=final),
        grid=(nb, n // tm),
        in_specs=[pl.BlockSpec((1, tm, D_MODEL), row),
                  pl.BlockSpec((1, tm, MLA_WIDTH), row), pl.BlockSpec((1, tm, SWA_WIDTH), row),
                  pl.BlockSpec((1, tm, HY_WIDTH), row), pl.BlockSpec((1, tm, HY_WIDTH), row),
                  pl.BlockSpec((1, tm, HY_WIDTH), row),
                  pl.BlockSpec((1, tm, D_MODEL), row), pl.BlockSpec((1, 1, D_MODEL), per_b),
                  pl.BlockSpec((1, HY_WIDTH), const), pl.BlockSpec((D_MIX, D_MODEL), const),
                  pl.BlockSpec((1, D_MODEL), const)],
        out_specs=pl.BlockSpec((1, tm, D_MODEL), row),
        out_shape=jax.ShapeDtypeStruct((nb, n, D_MODEL), F32),
        compiler_params=pltpu.CompilerParams(dimension_semantics=("parallel", "parallel")),
        name="merge",
    )(x, oa, ob, x0, z, yc, gates, gate_x, bias, w_out, fg)


def _layer(x, xc, mod, lp, consts, update_ctx, final, final_g):
    nb, seq, _ = x.shape
    nctx = xc.shape[1]
    w_aug = _prep_w_in(lp["w_in"])
    wuq = _prep_w_uq(lp["mla_w_uq"])
    wukv = _prep_w_ukv(lp["mla_w_ukv"])
    r1 = lambda a: a.reshape(1, -1)
    shift, scale, gate = (mod[:, D_MODEL * j:D_MODEL * (j + 1)] for j in range(3))
    sel = lambda a, lo, hi: a[lo:hi].reshape(hi - lo, 1, D_MODEL)
    ctx_b = lambda a: jnp.broadcast_to(a[nb:nb + 1].reshape(1, 1, D_MODEL), (nb, 1, D_MODEL))

    tm = min(seq, 512)
    px = _proj_call(x, sel(shift, 0, nb), sel(scale, 0, nb), r1(lp["norm_g"]), w_aug, r1(lp["mla_q_norm"]), wuq,
                    r1(lp["mla_kv_norm"]), wukv, *consts["rope_lat"], tm)
    pc = _proj_call(xc, ctx_b(shift), ctx_b(scale), r1(lp["norm_g"]), w_aug, r1(lp["mla_q_norm"]), wuq,
                    r1(lp["mla_kv_norm"]), wukv, *consts["rope_ctx"], nctx)
    qm, km, vm, qs, ks, vs, gates, hu = px
    qm_c, km_c, vm_c, qs_c, ks_c, vs_c, gates_c, hu_c = pc

    o_a = _mla_call(qm, km, vm, km_c, vm_c, min(seq, 256), min(seq, 512))
    o_b = _swa_call(lp["swa_sink"], qs, ks, vs, ks_c, vs_c, True)

    hf, hb, nrm = _filter_call(lp, seq)
    x0, z = _hconv_call(hu, lp["hy_conv_w"], lp["hy_conv_b"], min(seq, 1024))
    yc = _long_conv(z, hf, hb, nrm, consts["dft"])

    x_new = _merge_call(x, o_a, o_b, x0, z, yc, gates, sel(gate, 0, nb), r1(lp["hy_bias"]),
                        lp["w_out"].astype(BF16), r1(final_g), tm, final)
    if update_ctx:
        oc_a = _mla_call(qm_c, None, None, km_c, vm_c, nctx, nctx)
        oc_b = _swa_call(lp["swa_sink"], qs_c, None, None, ks_c, vs_c, False)
        hf_c, hb_c, nrm_c = _filter_call(lp, nctx)
        x0_c, z_c, yc_c = _hctx_call(hu_c, lp["hy_conv_w"], lp["hy_conv_b"], hf_c, hb_c, nrm_c)
        xc = _merge_call(xc, oc_a, oc_b, x0_c, z_c, yc_c, gates_c, ctx_b(gate), r1(lp["hy_bias"]),
                         lp["w_out"].astype(BF16), r1(final_g), nctx, False)
    return x_new, xc


def _forward(x, c, ctx, c_ctx, params, final_norm_g):
    nb, seq, _ = x.shape
    nctx = ctx.shape[1]
    cm, sm = _mla_rope_lanes(*_rope_tables(seq, MLA_ROPE))
    cs, ss = _swa_rope_lanes(*_rope_tables(seq, SWA_DIM))
    ones = jnp.ones((nctx, LANES), F32)
    zeros = jnp.zeros((nctx, LANES), F32)
    lane = jnp.arange(LANES)[None, :]
    cm_c = jnp.where(lane < MLA_QK, ones, zeros)
    consts = {"rope_lat": (cm, sm, cs, ss), "rope_ctx": (cm_c, zeros, ones, zeros), "dft": _dft_tables(seq)}
    cc = jnp.concatenate([c, c_ctx[None, :], jnp.zeros((8 - nb - 1, D_MODEL), F32)], axis=0)
    xc = ctx
    depth = params["w_in"].shape[0]
    for l in range(depth):
        lp = {k: v[l] for k, v in params.items()}
        mod = _mod_call(cc, lp["mod_w"], lp["mod_b"])
        x, xc = _layer(x, xc, mod, lp, consts, l < depth - 1, l == depth - 1, final_norm_g)
    return x


def kernel(x, c, ctx, c_ctx, norm_g, mod_w, mod_b, w_in, mla_q_norm, mla_w_uq, mla_kv_norm, mla_w_ukv, swa_sink,
           hy_conv_w, hy_conv_b, hy_w1, hy_b1, hy_freq, hy_w2, hy_b2, hy_w3, hy_b3, hy_bias, w_out, final_norm_g):
    params = dict(norm_g=norm_g, mod_w=mod_w, mod_b=mod_b, w_in=w_in, mla_q_norm=mla_q_norm, mla_w_uq=mla_w_uq,
                  mla_kv_norm=mla_kv_norm, mla_w_ukv=mla_w_ukv, swa_sink=swa_sink, hy_conv_w=hy_conv_w,
                  hy_conv_b=hy_conv_b, hy_w1=hy_w1, hy_b1=hy_b1, hy_freq=hy_freq, hy_w2=hy_w2, hy_b2=hy_b2,
                  hy_w3=hy_w3, hy_b3=hy_b3, hy_bias=hy_bias, w_out=w_out)
    return _forward(x, c, ctx, c_ctx, params, final_norm_g)
```

```python
import functools
import math

import numpy as np
import jax
import jax.numpy as jnp
from jax import lax
from jax.experimental import pallas as pl
from jax.experimental.pallas import tpu as pltpu

F32 = jnp.float32
BF16 = jnp.bfloat16
HI = lax.Precision.HIGHEST

D_MODEL = 1024
DEPTH = 2
GRID_W = 64
NORM_EPS = 1e-6
ROPE_THETA = 10000.0

MLA_HEADS = 6
MLA_NOPE = 64
MLA_ROPE = 32
MLA_QK = MLA_NOPE + MLA_ROPE
MLA_V = 64
MLA_Q_RANK = 384
MLA_KV_RANK = 256
MLA_WIDTH = MLA_HEADS * MLA_V

SWA_HEADS = 6
SWA_KV_HEADS = 2
SWA_GROUP = SWA_HEADS // SWA_KV_HEADS
SWA_DIM = 64
SWA_WINDOW = 128
SWA_WIDTH = SWA_HEADS * SWA_DIM

HY_WIDTH = 256
HY_BANDS = 8
HY_EMB = 1 + 2 * HY_BANDS
HY_HIDDEN = 64
HY_DECAY_FAST = math.log(1e-2) / 0.3
HY_DECAY_SLOW = math.log(1e-2) / 1.5

D_MIX = MLA_WIDTH + SWA_WIDTH + HY_WIDTH
IN_SIZES = (MLA_Q_RANK, MLA_KV_RANK, MLA_ROPE, MLA_WIDTH, SWA_WIDTH, SWA_KV_HEADS * SWA_DIM,
            SWA_KV_HEADS * SWA_DIM, SWA_WIDTH, 3 * HY_WIDTH, HY_WIDTH)
IN_OFFS = tuple(int(v) for v in np.concatenate([[0], np.cumsum(IN_SIZES)]))

LANES = 128
FFT_N2 = 128

SEG_SIZES = (("mq", 384), ("mkv", 256), ("kr", 256), ("mg", 384), ("sq", 768), ("sk", 512),
             ("sv", 256), ("sg", 384), ("hu", 768), ("hg", 256))
SEG = {}
_o = 0
for _n, _s in SEG_SIZES:
    SEG[_n] = (_o, _o + _s)
    _o += _s
AUG_COLS = _o


def _silu(v):
    return v * jax.nn.sigmoid(v)


def _rms(v, g):
    return v * lax.rsqrt(jnp.mean(v * v, axis=-1, keepdims=True) + NORM_EPS) * g


def _rot_partner(w, half):
    return jnp.concatenate([-w[:, half:], w[:, :half]], axis=1)


def _prep_w_in(w_in):
    o = IN_OFFS
    mq, mkv, mkr, mg = (w_in[:, o[i]:o[i + 1]] for i in range(4))
    sq, sk, sv, sg, hu, hg = (w_in[:, o[i]:o[i + 1]] for i in range(4, 10))
    z = lambda n: jnp.zeros((D_MODEL, n), w_in.dtype)
    kr = jnp.concatenate([z(MLA_NOPE), mkr, z(32), z(MLA_NOPE), _rot_partner(mkr, MLA_ROPE // 2), z(32)], axis=1)
    sq_p = jnp.concatenate([_rot_partner(sq[:, SWA_DIM * h:SWA_DIM * (h + 1)], SWA_DIM // 2)
                            for h in range(SWA_HEADS)], axis=1)
    kh = [sk[:, SWA_DIM * g:SWA_DIM * (g + 1)] for g in range(SWA_KV_HEADS)]
    sk2 = jnp.concatenate([kh[0], kh[0], kh[1], kh[1]], axis=1)
    khp = [_rot_partner(k, SWA_DIM // 2) for k in kh]
    sk2_p = jnp.concatenate([khp[0], khp[0], khp[1], khp[1]], axis=1)
    vh = [sv[:, SWA_DIM * g:SWA_DIM * (g + 1)] for g in range(SWA_KV_HEADS)]
    sv2 = jnp.concatenate([vh[0], vh[0], vh[1], vh[1]], axis=1)
    w = jnp.concatenate([mq, mkv, kr, mg, sq, sq_p, sk2, sk2_p, sv2, sg, hu, hg], axis=1)
    assert w.shape[1] == AUG_COLS
    return w.astype(BF16)


def _prep_w_uq(w_uq):
    zr = lambda n: jnp.zeros((MLA_Q_RANK, n), w_uq.dtype)
    main, part = [], []
    for h in range(MLA_HEADS):
        nope = w_uq[:, MLA_QK * h:MLA_QK * h + MLA_NOPE]
        rope = w_uq[:, MLA_QK * h + MLA_NOPE:MLA_QK * (h + 1)]
        main += [nope, rope, zr(32)]
        part += [zr(MLA_NOPE), _rot_partner(rope, MLA_ROPE // 2), zr(32)]
    return jnp.concatenate(main + part, axis=1).astype(BF16)


def _prep_w_ukv(w_ukv):
    zr = lambda n: jnp.zeros((MLA_KV_RANK, n), w_ukv.dtype)
    ks, vs = [], []
    for h in range(MLA_HEADS):
        ks += [w_ukv[:, 128 * h:128 * h + MLA_NOPE], zr(64)]
        vs += [w_ukv[:, 128 * h + MLA_NOPE:128 * (h + 1)]]
    return jnp.concatenate(ks + vs, axis=1).astype(BF16)


def _rope_tables(num_tokens, rot_dim):
    rows = num_tokens // GRID_W
    row = jnp.repeat(jnp.arange(rows, dtype=F32), GRID_W)
    col = jnp.tile(jnp.arange(GRID_W, dtype=F32), rows)
    n_freq = rot_dim // 4
    freqs = ROPE_THETA ** (-jnp.arange(n_freq, dtype=F32) / n_freq)
    ang = jnp.concatenate([row[:, None] * freqs, col[:, None] * freqs], axis=-1)
    return jnp.cos(ang), jnp.sin(ang)


def _mla_rope_lanes(cos, sin):
    n = cos.shape[0]
    one, zero = jnp.ones((n, MLA_NOPE), F32), jnp.zeros((n, 32), F32)
    cm = jnp.concatenate([one, cos, cos, zero], axis=1)
    sm = jnp.concatenate([jnp.zeros((n, MLA_NOPE), F32), sin, sin, zero], axis=1)
    return cm, sm


def _swa_rope_lanes(cos, sin):
    return jnp.concatenate([cos] * 4, axis=1), jnp.concatenate([sin] * 4, axis=1)


def _mod_kernel(c_ref, w_ref, b_ref, o_ref):
    s = _silu(c_ref[...])
    o_ref[...] = jnp.dot(s, w_ref[...], precision=HI, preferred_element_type=F32) + b_ref[...]


def _mod_call(cc, mod_w, mod_b):
    rows, n = cc.shape[0], mod_w.shape[1]
    tn = 512
    return pl.pallas_call(
        _mod_kernel,
        grid=(n // tn,),
        in_specs=[pl.BlockSpec((rows, D_MODEL), lambda j: (0, 0)),
                  pl.BlockSpec((D_MODEL, tn), lambda j: (0, j)),
                  pl.BlockSpec((1, tn), lambda j: (0, j))],
        out_specs=pl.BlockSpec((rows, tn), lambda j: (0, j)),
        out_shape=jax.ShapeDtypeStruct((rows, n), F32),
        name="mod",
    )(cc, mod_w, mod_b.reshape(1, n))


def _proj_kernel(x_ref, shift_ref, scale_ref, g_ref, w_ref, gq_ref, wuq_ref, gkv_ref, wukv_ref,
                 cm_ref, sm_ref, cs_ref, ss_ref,
                 qm_ref, km_ref, vm_ref, qs_ref, ks_ref, vs_ref, gate_ref, hu_ref):
    x = x_ref[0]
    y = _rms(x, g_ref[...])
    h = (y * (1.0 + scale_ref[0]) + shift_ref[0]).astype(BF16)

    def seg(name):
        a, b = SEG[name]
        return jnp.dot(h, w_ref[:, a:b], preferred_element_type=F32)

    cm, sm = cm_ref[...], sm_ref[...]
    cs, ss = cs_ref[...], ss_ref[...]

    qn = _rms(seg("mq"), gq_ref[...]).astype(BF16)
    q2 = jnp.dot(qn, wuq_ref[...], preferred_element_type=F32)
    qscale = MLA_QK ** -0.5 * math.log2(math.e)
    half = MLA_HEADS * LANES
    for hd in range(MLA_HEADS):
        a = q2[:, LANES * hd:LANES * (hd + 1)]
        b = q2[:, half + LANES * hd:half + LANES * (hd + 1)]
        qm_ref[0, :, LANES * hd:LANES * (hd + 1)] = ((a * cm + b * sm) * qscale).astype(BF16)

    kvn = _rms(seg("mkv"), gkv_ref[...]).astype(BF16)
    kv = jnp.dot(kvn, wukv_ref[...], preferred_element_type=F32)
    kr = seg("kr")
    kr = kr[:, :LANES] * cm + kr[:, LANES:] * sm
    for hd in range(MLA_HEADS):
        km_ref[0, :, LANES * hd:LANES * (hd + 1)] = (kv[:, LANES * hd:LANES * (hd + 1)] + kr).astype(BF16)
    vm_ref[0] = kv[:, half:].T.astype(BF16)

    sq = seg("sq")
    for j in range(3):
        a = sq[:, LANES * j:LANES * (j + 1)]
        b = sq[:, 384 + LANES * j:384 + LANES * (j + 1)]
        qs_ref[0, :, LANES * j:LANES * (j + 1)] = ((a * cs + b * ss) * (SWA_DIM ** -0.5)).astype(BF16)
    sk = seg("sk")
    for j in range(2):
        a = sk[:, LANES * j:LANES * (j + 1)]
        b = sk[:, 256 + LANES * j:256 + LANES * (j + 1)]
        ks_ref[0, :, LANES * j:LANES * (j + 1)] = (a * cs + b * ss).astype(BF16)
    vs_ref[0] = seg("sv").astype(BF16)

    gate_ref[0, :, 0:384] = _silu(seg("mg")).astype(BF16)
    gate_ref[0, :, 384:768] = _silu(seg("sg")).astype(BF16)
    gate_ref[0, :, 768:1024] = _silu(seg("hg")).astype(BF16)
    hu_ref[0] = seg("hu")


def _proj_call(x, shift, scale, norm_g, w_aug, gq, wuq, gkv, wukv, cm, sm, cs, ss, tm):
    nb, n, _ = x.shape
    row = lambda b, i: (b, i, 0)
    per_b = lambda b, i: (b, 0, 0)
    const = lambda b, i: (0, 0)
    tab = lambda b, i: (i, 0)
    widths = (768, 768, 384, 384, 256, 256, 1024, 768)
    dtypes = (BF16,) * 7 + (F32,)
    return pl.pallas_call(
        _proj_kernel,
        grid=(nb, n // tm),
        in_specs=[pl.BlockSpec((1, tm, D_MODEL), row),
                  pl.BlockSpec((1, 1, D_MODEL), per_b), pl.BlockSpec((1, 1, D_MODEL), per_b),
                  pl.BlockSpec((1, D_MODEL), const),
                  pl.BlockSpec((D_MODEL, AUG_COLS), const),
                  pl.BlockSpec((1, MLA_Q_RANK), const), pl.BlockSpec(wuq.shape, const),
                  pl.BlockSpec((1, MLA_KV_RANK), const), pl.BlockSpec(wukv.shape, const),
                  pl.BlockSpec((tm, LANES), tab), pl.BlockSpec((tm, LANES), tab),
                  pl.BlockSpec((tm, LANES), tab), pl.BlockSpec((tm, LANES), tab)],
        out_specs=[pl.BlockSpec((1, MLA_WIDTH, tm), lambda b, i: (b, 0, i)) if j == 2 else pl.BlockSpec((1, tm, w), row)
                   for j, w in enumerate(widths)],
        out_shape=[jax.ShapeDtypeStruct((nb, MLA_WIDTH, n) if j == 2 else (nb, n, w), d)
                   for j, (w, d) in enumerate(zip(widths, dtypes))],
        compiler_params=pltpu.CompilerParams(dimension_semantics=("parallel", "parallel")),
        name="proj",
    )(x, shift, scale, norm_g, w_aug, gq, wuq, gkv, wukv, cm, sm, cs, ss)


def _sublane_all(v, op):
    for sh in (4, 2, 1):
        v = op(v, pltpu.roll(v, sh, 0))
    return v


def _mla_kernel(*refs, n_lat, tk, has_lat):
    if has_lat:
        q_ref, kl_ref, vtl_ref, kc_ref, vtc_ref, o_ref, s_sc, mx_sc, m_sc, l_sc, acc_sc = refs
    else:
        q_ref, kc_ref, vtc_ref, o_ref, s_sc, mx_sc, m_sc, l_sc, acc_sc = refs
    tq = q_ref.shape[1]
    nc = kc_ref.shape[1]
    nt = (((1,), (1,)), ((), ()))
    heads = range(2)
    m_sc[...] = jnp.full(m_sc.shape, -1e30, F32)
    l_sc[...] = jnp.zeros(l_sc.shape, F32)
    acc_sc[...] = jnp.zeros(acc_sc.shape, F32)

    def scores(slot, hh, k):
        n = k.shape[0]
        q = q_ref[0, :, LANES * hh:LANES * (hh + 1)]
        s = lax.dot_general(k, q, nt, preferred_element_type=F32)
        s_sc[slot, hh, 0:n, :] = s
        mx_sc[slot, hh] = jnp.max(s.reshape(n // 8, 8, tq), axis=0)

    def update(slot, hh, vt):
        n = vt.shape[1]
        m_prev = m_sc[hh]
        m_new = jnp.maximum(m_prev, _sublane_all(mx_sc[slot, hh], jnp.maximum))
        alpha = jnp.exp2(m_prev - m_new)
        p3 = jnp.exp2(s_sc[slot, hh, 0:n, :].reshape(n // 8, 8, tq) - m_new[None])
        l_sc[hh] = alpha * l_sc[hh] + jnp.sum(p3, axis=0)
        pv = jnp.dot(vt, p3.reshape(n, tq).astype(BF16), preferred_element_type=F32)
        acc_sc[hh] = acc_sc[hh] * jnp.tile(alpha, (LANES // 8, 1)) + pv
        m_sc[hh] = m_new

    def k_lat(i, hh):
        return kl_ref[0, pl.ds(pl.multiple_of(i * tk, tk), tk), LANES * hh:LANES * (hh + 1)]

    def vt_lat(i):
        return vtl_ref[0, :, pl.ds(pl.multiple_of(i * tk, tk), tk)]

    def k_ctx(hh):
        return kc_ref[0, :, LANES * hh:LANES * (hh + 1)]

    n = n_lat // tk if has_lat else 0
    if n == 0:
        for hh in heads:
            scores(0, hh, k_ctx(hh))
    else:
        for hh in heads:
            scores(0, hh, k_lat(0, hh))
        if n % 2 == 0 and n >= 4:
            def body(j, carry):
                i0 = 2 * j
                for hh in heads:
                    scores(1, hh, k_lat(i0 + 1, hh))
                for hh in heads:
                    update(0, hh, vt_lat(i0))
                for hh in heads:
                    scores(0, hh, k_lat(i0 + 2, hh))
                for hh in heads:
                    update(1, hh, vt_lat(i0 + 1))
                return carry
            lax.fori_loop(0, n // 2 - 1, body, 0)
            first = n - 2
        else:
            first = 0
        for i in range(first, n):
            for hh in heads:
                if i + 1 < n:
                    scores((i + 1) % 2, hh, k_lat(i + 1, hh))
                else:
                    scores((i + 1) % 2, hh, k_ctx(hh))
            for hh in heads:
                update(i % 2, hh, vt_lat(i))
    for hh in heads:
        update(n % 2, hh, vtc_ref[0])
    outs = []
    for hh in range(2):
        inv = 1.0 / _sublane_all(l_sc[hh], jnp.add)
        outs.append(acc_sc[hh] * jnp.tile(inv, (LANES // 8, 1)))
    row = lax.broadcasted_iota(jnp.int32, (LANES, tq), 0)
    o_ref[0] = jnp.where(row < MLA_V, outs[0], outs[1]).T


def _mla_call(q, k_lat, vt_lat, k_ctx, vt_ctx, tq, tk):
    nb, nq, _ = q.shape
    nc = k_ctx.shape[1]
    has_lat = k_lat is not None
    qmap = lambda b, p, i: (b, i, p)
    kvmap = lambda b, p, i: (b, 0, p)
    vtmap = lambda b, p, i: (b, p, 0)
    in_specs = [pl.BlockSpec((1, tq, 2 * LANES), qmap)]
    args = [q]
    n_lat = 0
    if has_lat:
        n_lat = k_lat.shape[1]
        in_specs += [pl.BlockSpec((1, n_lat, 2 * LANES), kvmap), pl.BlockSpec((1, LANES, n_lat), vtmap)]
        args += [k_lat, vt_lat]
    in_specs += [pl.BlockSpec((1, nc, 2 * LANES), kvmap), pl.BlockSpec((1, LANES, nc), vtmap)]
    args += [k_ctx, vt_ctx]
    return pl.pallas_call(
        functools.partial(_mla_kernel, n_lat=n_lat, tk=tk, has_lat=has_lat),
        grid=(nb, MLA_HEADS // 2, nq // tq),
        in_specs=in_specs,
        out_specs=pl.BlockSpec((1, tq, LANES), qmap),
        out_shape=jax.ShapeDtypeStruct((nb, nq, MLA_WIDTH), F32),
        scratch_shapes=[pltpu.VMEM((2, 2, max(tk, nc), tq), F32), pltpu.VMEM((2, 2, 8, tq), F32),
                        pltpu.VMEM((2, 8, tq), F32), pltpu.VMEM((2, 8, tq), F32), pltpu.VMEM((2, LANES, tq), F32)],
        compiler_params=pltpu.CompilerParams(dimension_semantics=("parallel", "parallel", "parallel")),
        name="mla_lat" if has_lat else "mla_ctx",
    )(*args)


def _swa_kernel(*refs, seq, band):
    if band:
        sink_ref, q_ref, kp_ref, kc_ref, kn_ref, vp_ref, vc_ref, vn_ref, kx_ref, vx_ref, o_ref = refs
    else:
        sink_ref, q_ref, kx_ref, vx_ref, o_ref = refs
    tq = q_ref.shape[1]
    nt = (((1,), (1,)), ((), ()))
    lane = lax.broadcasted_iota(jnp.int32, (tq, LANES), 1)
    lo = lane < SWA_DIM
    if band:
        i = pl.program_id(1)
        kband = jnp.concatenate([kp_ref[0], kc_ref[0], kn_ref[0]], axis=0)
        vband = jnp.concatenate([vp_ref[0], vc_ref[0], vn_ref[0]], axis=0)
        qpos = i * tq + lax.broadcasted_iota(jnp.int32, (tq, 3 * tq), 0)
        kpos = (i - 1) * tq + lax.broadcasted_iota(jnp.int32, (tq, 3 * tq), 1)
        valid = (jnp.abs(kpos - qpos) <= SWA_WINDOW) & (kpos >= 0) & (kpos < seq)
    kx, vx = kx_ref[0], vx_ref[0]
    for pr in range(SWA_HEADS // 2):
        qp = q_ref[0, :, LANES * pr:LANES * (pr + 1)]
        res = []
        for hh in range(2):
            hd = 2 * pr + hh
            g = hd // SWA_GROUP
            q = jnp.where(lo if hh == 0 else jnp.logical_not(lo), qp, jnp.zeros_like(qp))
            sink = sink_ref[hd]
            s_ctx = lax.dot_general(q, kx[:, LANES * g:LANES * (g + 1)], nt, preferred_element_type=F32)
            m = jnp.maximum(jnp.max(s_ctx, axis=1, keepdims=True), sink)
            if band:
                s_loc = lax.dot_general(q, kband[:, LANES * g:LANES * (g + 1)], nt, preferred_element_type=F32)
                s_loc = jnp.where(valid, s_loc, -jnp.inf)
                m = jnp.maximum(m, jnp.max(s_loc, axis=1, keepdims=True))
            p_ctx = jnp.exp(s_ctx - m)
            denom = jnp.sum(p_ctx, axis=1, keepdims=True) + jnp.exp(sink - m)
            o = jnp.dot(p_ctx.astype(BF16), vx[:, LANES * g:LANES * (g + 1)], preferred_element_type=F32)
            if band:
                p_loc = jnp.exp(s_loc - m)
                denom = denom + jnp.sum(p_loc, axis=1, keepdims=True)
                o = o + jnp.dot(p_loc.astype(BF16), vband[:, LANES * g:LANES * (g + 1)], preferred_element_type=F32)
            res.append(o / denom)
        o_ref[0, :, LANES * pr:LANES * (pr + 1)] = jnp.where(lo, res[0], res[1])


def _swa_call(sink, q, k, v, k_ctx, v_ctx, band):
    nb, nq, _ = q.shape
    nc = k_ctx.shape[1]
    tq = SWA_WINDOW if band else nq
    nblk = nq // tq
    cur = lambda b, i: (b, i, 0)
    prev = lambda b, i: (b, jnp.maximum(i - 1, 0), 0)
    nxt = lambda b, i: (b, jnp.minimum(i + 1, nblk - 1), 0)
    ctx = lambda b, i: (b, 0, 0)
    kw = 2 * LANES
    in_specs = [pl.BlockSpec(memory_space=pltpu.SMEM), pl.BlockSpec((1, tq, SWA_WIDTH), cur)]
    args = [sink, q]
    if band:
        in_specs += [pl.BlockSpec((1, tq, kw), prev), pl.BlockSpec((1, tq, kw), cur), pl.BlockSpec((1, tq, kw), nxt)] * 2
        args += [k, k, k, v, v, v]
    in_specs += [pl.BlockSpec((1, nc, kw), ctx), pl.BlockSpec((1, nc, kw), ctx)]
    args += [k_ctx, v_ctx]
    return pl.pallas_call(
        functools.partial(_swa_kernel, seq=nq, band=band),
        grid=(nb, nblk),
        in_specs=in_specs,
        out_specs=pl.BlockSpec((1, tq, SWA_WIDTH), cur),
        out_shape=jax.ShapeDtypeStruct((nb, nq, SWA_WIDTH), F32),
        compiler_params=pltpu.CompilerParams(dimension_semantics=("parallel", "parallel")),
        name="swa_lat" if band else "swa_ctx",
    )(*args)


def _filter_kernel(z_ref, w1_ref, b1_ref, fr_ref, w2_ref, b2_ref, w3_ref, b3_ref, dl_ref,
                   hf_ref, hb_ref, nrm_ref, *, num_tokens):
    i = pl.program_id(0)
    tl = z_ref.shape[0]
    fr = fr_ref[...]
    h = jnp.sin(fr * (jnp.dot(z_ref[...], w1_ref[...], precision=HI, preferred_element_type=F32) + b1_ref[...]))
    h = jnp.sin(fr * (jnp.dot(h, w2_ref[...], precision=HI, preferred_element_type=F32) + b2_ref[...]))
    h = jnp.dot(h, w3_ref[...], precision=HI, preferred_element_type=F32) + b3_ref[...]
    row = i * tl + lax.broadcasted_iota(jnp.int32, (tl, HY_WIDTH), 0)
    t = row.astype(F32) * (1.0 / (num_tokens - 1))
    decay = jnp.exp(-t * dl_ref[...])
    hf = h[:, :HY_WIDTH] * decay
    hb = jnp.where(row == 0, 0.0, h[:, HY_WIDTH:] * decay)
    hf_ref[...] = hf
    hb_ref[...] = hb
    part = jnp.sum(jnp.abs(hf) + jnp.abs(hb), axis=0, keepdims=True)

    @pl.when(i == 0)
    def _():
        nrm_ref[...] = jnp.zeros(nrm_ref.shape, F32)

    nrm_ref[...] += part


def _filter_call(lp, num_tokens):
    tl = min(num_tokens, 1024)
    t = jnp.linspace(0.0, 1.0, num_tokens, dtype=F32)[:, None]
    w = (2.0 * math.pi / num_tokens) * jnp.arange(num_tokens, dtype=F32)[:, None]
    bands = jnp.linspace(1e-4, HY_BANDS - 1, HY_BANDS, dtype=F32)[None, :]
    z = jnp.concatenate([t, jnp.cos(bands * w), -jnp.sin(bands * w),
                         jnp.zeros((num_tokens, LANES - HY_EMB), F32)], axis=-1)
    w1 = jnp.concatenate([lp["hy_w1"], jnp.zeros((LANES - HY_EMB, HY_HIDDEN), F32)], axis=0)
    deltas = jnp.abs(jnp.linspace(HY_DECAY_FAST, HY_DECAY_SLOW, HY_WIDTH, dtype=F32)).reshape(1, HY_WIDTH)
    const = lambda i: (0, 0)
    rowm = lambda i: (i, 0)
    r1 = lambda a: a.reshape(1, -1)
    return pl.pallas_call(
        functools.partial(_filter_kernel, num_tokens=num_tokens),
        grid=(num_tokens // tl,),
        in_specs=[pl.BlockSpec((tl, LANES), rowm),
                  pl.BlockSpec((LANES, HY_HIDDEN), const), pl.BlockSpec((1, HY_HIDDEN), const),
                  pl.BlockSpec((1, HY_HIDDEN), const),
                  pl.BlockSpec((HY_HIDDEN, HY_HIDDEN), const), pl.BlockSpec((1, HY_HIDDEN), const),
                  pl.BlockSpec((HY_HIDDEN, 2 * HY_WIDTH), const), pl.BlockSpec((1, 2 * HY_WIDTH), const),
                  pl.BlockSpec((1, HY_WIDTH), const)],
        out_specs=[pl.BlockSpec((tl, HY_WIDTH), rowm), pl.BlockSpec((tl, HY_WIDTH), rowm),
                   pl.BlockSpec((1, HY_WIDTH), const)],
        out_shape=[jax.ShapeDtypeStruct((num_tokens, HY_WIDTH), F32)] * 2 + [jax.ShapeDtypeStruct((1, HY_WIDTH), F32)],
        compiler_params=pltpu.CompilerParams(dimension_semantics=("arbitrary",)),
        name="hy_filter",
    )(z, w1, r1(lp["hy_b1"]), r1(lp["hy_freq"]), lp["hy_w2"], r1(lp["hy_b2"]), lp["hy_w3"], r1(lp["hy_b3"]), deltas)


def _short_conv(u, prev_row, next_row, w_ref, b_ref):
    n = u.shape[0]
    row = lax.broadcasted_iota(jnp.int32, u.shape, 0)
    up = jnp.where(row == 0, prev_row, pltpu.roll(u, 1, 0))
    un = jnp.where(row == n - 1, next_row, pltpu.roll(u, n - 1, 0))
    return up * w_ref[0:1, :] + u * w_ref[1:2, :] + un * w_ref[2:3, :] + b_ref[...]


def _hconv_kernel(u_ref, up_ref, un_ref, w_ref, b_ref, x0_ref, z_ref):
    i = pl.program_id(1)
    last = pl.num_programs(1) - 1
    prev_row = jnp.where(i == 0, 0.0, up_ref[0, 7:8, :])
    next_row = jnp.where(i == last, 0.0, un_ref[0, 0:1, :])
    uc = _short_conv(u_ref[0], prev_row, next_row, w_ref, b_ref)
    x0_ref[0] = uc[:, :HY_WIDTH]
    z_ref[0] = uc[:, 2 * HY_WIDTH:] * uc[:, HY_WIDTH:2 * HY_WIDTH]


def _hconv_call(hu, conv_w, conv_b, tl):
    nb, n, w3 = hu.shape
    hb = tl // 8
    nh = n // 8
    cur = lambda b, i: (b, i, 0)
    prev = lambda b, i: (b, jnp.maximum(i * hb - 1, 0), 0)
    nxt = lambda b, i: (b, jnp.minimum((i + 1) * hb, nh - 1), 0)
    const = lambda b, i: (0, 0)
    return pl.pallas_call(
        _hconv_kernel,
        grid=(nb, n // tl),
        in_specs=[pl.BlockSpec((1, tl, w3), cur), pl.BlockSpec((1, 8, w3), prev), pl.BlockSpec((1, 8, w3), nxt),
                  pl.BlockSpec((3, w3), const), pl.BlockSpec((1, w3), const)],
        out_specs=[pl.BlockSpec((1, tl, HY_WIDTH), cur)] * 2,
        out_shape=[jax.ShapeDtypeStruct((nb, n, HY_WIDTH), F32)] * 2,
        compiler_params=pltpu.CompilerParams(dimension_semantics=("parallel", "parallel")),
        name="hy_conv",
    )(hu, hu, hu, conv_w, conv_b.reshape(1, w3))


def _dft_tables(seq):
    n1 = 2 * seq // FFT_N2
    nd = seq // FFT_N2
    n = n1 * FFT_N2
    k1 = np.arange(n1)[:, None]
    a = 2.0 * np.pi * ((k1 * np.arange(nd)[None, :]) % n1) / n1
    c, s = np.cos(a), np.sin(a)
    fwd1 = np.block([[c, s], [-s, c]])
    inv1 = np.block([[c.T, -s.T], [s.T, c.T]])
    k1j = jnp.arange(n1, dtype=jnp.int32)[:, None, None]
    k2 = jnp.arange(FFT_N2, dtype=jnp.int32)[None, :, None]
    n2 = jnp.arange(FFT_N2, dtype=jnp.int32)[None, None, :]
    ph = (n2 * (k1j + n1 * k2)) % n
    g = ph.astype(F32) * (2.0 * math.pi / n)
    gr, gi = jnp.cos(g), -jnp.sin(g)
    cast = lambda t: jnp.asarray(t, BF16)
    return (cast(fwd1), cast(inv1), cast(gr), cast(gi),
            cast(jnp.transpose(gr, (0, 2, 1))), cast(jnp.transpose(-gi, (0, 2, 1))))


def _stage1_kernel(m_ref, z_ref, a_ref):
    zs = jnp.concatenate([z_ref[0, 0], z_ref[0, 1]], axis=0).astype(BF16)
    a_ref[0] = jnp.dot(m_ref[...], zs, preferred_element_type=F32).astype(a_ref.dtype)


def _stage1_call(mat, z, out_dtype, tn, name):
    npair, _, rows, cols = z.shape
    mo = mat.shape[0]
    return pl.pallas_call(
        _stage1_kernel,
        grid=(npair, cols // tn),
        in_specs=[pl.BlockSpec(mat.shape, lambda p, j: (0, 0)),
                  pl.BlockSpec((1, 2, rows, tn), lambda p, j: (p, 0, 0, j))],
        out_specs=pl.BlockSpec((1, mo, tn), lambda p, j: (p, 0, j)),
        out_shape=jax.ShapeDtypeStruct((npair, mo, cols), out_dtype),
        compiler_params=pltpu.CompilerParams(dimension_semantics=("parallel", "parallel")),
        name=name,
    )(mat, z)


def _cplx_mat(re, im):
    return jnp.concatenate([jnp.concatenate([re, -im], axis=1), jnp.concatenate([im, re], axis=1)], axis=0)


def _spec_kernel(gr_ref, gi_ref, a_ref, nrm_ref, k_ref, *, kb, scale):
    inv = scale / (nrm_ref[...] + NORM_EPS)
    for j in range(kb):
        g2 = _cplx_mat(gr_ref[j], gi_ref[j])
        x1 = jnp.dot(g2, jnp.concatenate([a_ref[0, 0, j], a_ref[0, 1, j]], axis=0), preferred_element_type=F32)
        x2 = jnp.dot(g2, jnp.concatenate([a_ref[1, 0, j], a_ref[1, 1, j]], axis=0), preferred_element_type=F32)
        k_ref[j, :FFT_N2, :] = (x1[:FFT_N2] + x2[:FFT_N2]) * inv
        k_ref[j, FFT_N2:, :] = (x1[FFT_N2:] - x2[FFT_N2:]) * inv


def _spec_call(gr, gi, a, nrm, kb, scale):
    n1 = gr.shape[0]
    tab = lambda i: (i, 0, 0)
    return pl.pallas_call(
        functools.partial(_spec_kernel, kb=kb, scale=scale),
        grid=(n1 // kb,),
        in_specs=[pl.BlockSpec((kb, FFT_N2, FFT_N2), tab), pl.BlockSpec((kb, FFT_N2, FFT_N2), tab),
                  pl.BlockSpec((2, 2, kb, FFT_N2, HY_WIDTH), lambda i: (0, 0, i, 0, 0)),
                  pl.BlockSpec((1, HY_WIDTH), lambda i: (0, 0))],
        out_specs=pl.BlockSpec((kb, 2 * FFT_N2, HY_WIDTH), tab),
        out_shape=jax.ShapeDtypeStruct((n1, 2 * FFT_N2, HY_WIDTH), F32),
        compiler_params=pltpu.CompilerParams(dimension_semantics=("parallel",)),
        name="hy_spec",
    )(gr, gi, a, nrm)


def _stage2_kernel(gr_ref, gi_ref, hr_ref, hi_ref, k_ref, a_ref, b_ref, *, kb):
    for j in range(kb):
        s = jnp.concatenate([a_ref[0, 0, j], a_ref[0, 1, j]], axis=0)
        x = jnp.dot(_cplx_mat(gr_ref[j], gi_ref[j]), s, preferred_element_type=F32)
        xr, xi = x[:FFT_N2], x[FFT_N2:]
        kr, ki = k_ref[j, :FFT_N2, :], k_ref[j, FFT_N2:, :]
        y = jnp.concatenate([xr * kr - xi * ki, xr * ki + xi * kr], axis=0).astype(BF16)
        b = jnp.dot(_cplx_mat(hr_ref[j], hi_ref[j]), y, preferred_element_type=F32)
        b_ref[0, 0, j] = b[:FFT_N2].astype(b_ref.dtype)
        b_ref[0, 1, j] = b[FFT_N2:].astype(b_ref.dtype)


def _stage2_call(gr, gi, hr, hi, kf, a, kb):
    npair = a.shape[0]
    n1 = gr.shape[0]
    tab = lambda p, i: (i, 0, 0)
    dat = lambda p, i: (p, 0, i, 0, 0)
    return pl.pallas_call(
        functools.partial(_stage2_kernel, kb=kb),
        grid=(npair, n1 // kb),
        in_specs=[pl.BlockSpec((kb, FFT_N2, FFT_N2), tab)] * 4
                 + [pl.BlockSpec((kb, 2 * FFT_N2, HY_WIDTH), tab),
                    pl.BlockSpec((1, 2, kb, FFT_N2, HY_WIDTH), dat)],
        out_specs=pl.BlockSpec((1, 2, kb, FFT_N2, HY_WIDTH), dat),
        out_shape=jax.ShapeDtypeStruct(a.shape, BF16),
        compiler_params=pltpu.CompilerParams(dimension_semantics=("parallel", "parallel")),
        name="hy_stage2",
    )(gr, gi, hr, hi, kf, a)


def _long_conv(z, hf, hb, nrm, tables):
    nb, seq, w = z.shape
    fwd1, inv1, gr, gi, hr, hi = tables
    n1 = 2 * seq // FFT_N2
    nd = seq // FFT_N2
    cols = FFT_N2 * w
    tn = min(cols, 4096)
    kb = min(n1, 8)
    zero = jnp.zeros_like(hf)
    zf = jnp.stack([jnp.stack([hf, zero]), jnp.stack([hb, zero])]).reshape(2, 2, nd, cols)
    af = _stage1_call(fwd1, zf, BF16, tn, "hy_fstage1").reshape(2, 2, n1, FFT_N2, w)
    kf = _spec_call(gr, gi, af, nrm, kb, 1.0 / (n1 * FFT_N2))
    zd = z.reshape(nb // 2, 2, nd, cols)
    a = _stage1_call(fwd1, zd, BF16, tn, "hy_stage1").reshape(nb // 2, 2, n1, FFT_N2, w)
    b = _stage2_call(gr, gi, hr, hi, kf, a, kb).reshape(nb // 2, 2, n1, cols)
    y = _stage1_call(inv1, b, F32, tn, "hy_istage1")
    return y.reshape(nb, seq, w)


def _hctx_kernel(u_ref, w_ref, b_ref, hf_ref, hb_ref, nrm_ref, c_ref, s_ref, ct_ref, st_ref,
                 x0_ref, z_ref, y_ref):
    u = u_ref[0]
    n = u.shape[0]
    zero_row = jnp.zeros((1, u.shape[1]), F32)
    uc = _short_conv(u, zero_row, zero_row, w_ref, b_ref)
    x0 = uc[:, :HY_WIDTH]
    z = uc[:, 2 * HY_WIDTH:] * uc[:, HY_WIDTH:2 * HY_WIDTH]
    dot = lambda a, b: jnp.dot(a, b, precision=HI, preferred_element_type=F32)
    c, s = c_ref[...], s_ref[...]
    inv = (1.0 / (2 * n)) / (nrm_ref[...] + NORM_EPS)
    kr = (dot(c, hf_ref[...]) + dot(c, hb_ref[...])) * inv
    ki = (dot(s, hb_ref[...]) - dot(s, hf_ref[...])) * inv
    xr, xi = dot(c, z), -dot(s, z)
    yr, yi = xr * kr - xi * ki, xr * ki + xi * kr
    x0_ref[0] = x0
    z_ref[0] = z
    y_ref[0] = dot(ct_ref[...], yr) - dot(st_ref[...], yi)


def _hctx_call(hu, conv_w, conv_b, hf, hb, nrm):
    nb, n, w3 = hu.shape
    k = np.arange(2 * n)[:, None]
    a = 2.0 * np.pi * ((k * np.arange(n)[None, :]) % (2 * n)) / (2 * n)
    c, s = np.cos(a).astype(np.float32), np.sin(a).astype(np.float32)
    const = lambda b: (0, 0)
    cur = lambda b: (b, 0, 0)
    out = pl.BlockSpec((1, n, HY_WIDTH), cur)
    return pl.pallas_call(
        _hctx_kernel,
        grid=(nb,),
        in_specs=[pl.BlockSpec((1, n, w3), cur), pl.BlockSpec((3, w3), const), pl.BlockSpec((1, w3), const),
                  pl.BlockSpec((n, HY_WIDTH), const), pl.BlockSpec((n, HY_WIDTH), const),
                  pl.BlockSpec((1, HY_WIDTH), const),
                  pl.BlockSpec((2 * n, n), const), pl.BlockSpec((2 * n, n), const),
                  pl.BlockSpec((n, 2 * n), const), pl.BlockSpec((n, 2 * n), const)],
        out_specs=[out, out, out],
        out_shape=[jax.ShapeDtypeStruct((nb, n, HY_WIDTH), F32)] * 3,
        compiler_params=pltpu.CompilerParams(dimension_semantics=("parallel",)),
        name="hy_ctx",
    )(hu, conv_w, conv_b.reshape(1, w3), hf, hb, nrm, jnp.asarray(c), jnp.asarray(s),
      jnp.asarray(c.T.copy()), jnp.asarray(s.T.copy()))


def _merge_kernel(x_ref, oa_ref, ob_ref, x0_ref, z_ref, yc_ref, gate_ref, gx_ref, bias_ref, w_ref, fg_ref,
                  o_ref, *, final):
    g = gate_ref[0].astype(F32)
    ya = oa_ref[0] * g[:, 0:384]
    yb = ob_ref[0] * g[:, 384:768]
    z = z_ref[0]
    yh = x0_ref[0] * (yc_ref[0] + bias_ref[...] * z) * g[:, 768:1024]
    y = jnp.concatenate([ya, yb, yh], axis=1).astype(BF16)
    xn = x_ref[0] + gx_ref[0] * jnp.dot(y, w_ref[...], preferred_element_type=F32)
    if final:
        xn = _rms(xn, fg_ref[...])
    o_ref[0] = xn


def _merge_call(x, oa, ob, x0, z, yc, gates, gate_x, bias, w_out, fg, tm, final):
    nb, n, _ = x.shape
    row = lambda b, i: (b, i, 0)
    per_b = lambda b, i: (b, 0, 0)
    const = lambda b, i: (0, 0)
    return pl.pallas_call(
        functools.partial(_merge_kernel, final=final),
        grid=(nb, n // tm),
        in_specs=[pl.BlockSpec((1, tm, D_MODEL), row),
                  pl.BlockSpec((1, tm, MLA_WIDTH), row), pl.BlockSpec((1, tm, SWA_WIDTH), row),
                  pl.BlockSpec((1, tm, HY_WIDTH), row), pl.BlockSpec((1, tm, HY_WIDTH), row),
                  pl.BlockSpec((1, tm, HY_WIDTH), row),
                  pl.BlockSpec((1, tm, D_MODEL), row), pl.BlockSpec((1, 1, D_MODEL), per_b),
                  pl.BlockSpec((1, HY_WIDTH), const), pl.BlockSpec((D_MIX, D_MODEL), const),
                  pl.BlockSpec((1, D_MODEL), const)],
        out_specs=pl.BlockSpec((1, tm, D_MODEL), row),
        out_shape=jax.ShapeDtypeStruct((nb, n, D_MODEL), F32),
        compiler_params=pltpu.CompilerParams(dimension_semantics=("parallel", "parallel")),
        name="merge",
    )(x, oa, ob, x0, z, yc, gates, gate_x, bias, w_out, fg)


def _layer(x, xc, mod, lp, consts, update_ctx, final, final_g):
    nb, seq, _ = x.shape
    nctx = xc.shape[1]
    w_aug = _prep_w_in(lp["w_in"])
    wuq = _prep_w_uq(lp["mla_w_uq"])
    wukv = _prep_w_ukv(lp["mla_w_ukv"])
    r1 = lambda a: a.reshape(1, -1)
    shift, scale, gate = (mod[:, D_MODEL * j:D_MODEL * (j + 1)] for j in range(3))
    sel = lambda a, lo, hi: a[lo:hi].reshape(hi - lo, 1, D_MODEL)
    ctx_b = lambda a: jnp.broadcast_to(a[nb:nb + 1].reshape(1, 1, D_MODEL), (nb, 1, D_MODEL))

    tm = min(seq, 512)
    px = _proj_call(x, sel(shift, 0, nb), sel(scale, 0, nb), r1(lp["norm_g"]), w_aug, r1(lp["mla_q_norm"]), wuq,
                    r1(lp["mla_kv_norm"]), wukv, *consts["rope_lat"], tm)
    pc = _proj_call(xc, ctx_b(shift), ctx_b(scale), r1(lp["norm_g"]), w_aug, r1(lp["mla_q_norm"]), wuq,
                    r1(lp["mla_kv_norm"]), wukv, *consts["rope_ctx"], nctx)
    qm, km, vm, qs, ks, vs, gates, hu = px
    qm_c, km_c, vm_c, qs_c, ks_c, vs_c, gates_c, hu_c = pc

    o_a = _mla_call(qm, km, vm, km_c, vm_c, min(seq, 256), 1024 if seq >= 2048 else 128)
    o_b = _swa_call(lp["swa_sink"], qs, ks, vs, ks_c, vs_c, True)

    hf, hb, nrm = _filter_call(lp, seq)
    x0, z = _hconv_call(hu, lp["hy_conv_w"], lp["hy_conv_b"], min(seq, 1024))
    yc = _long_conv(z, hf, hb, nrm, consts["dft"])

    x_new = _merge_call(x, o_a, o_b, x0, z, yc, gates, sel(gate, 0, nb), r1(lp["hy_bias"]),
                        lp["w_out"].astype(BF16), r1(final_g), tm, final)
    if update_ctx:
        oc_a = _mla_call(qm_c, None, None, km_c, vm_c, nctx, nctx)
        oc_b = _swa_call(lp["swa_sink"], qs_c, None, None, ks_c, vs_c, False)
        hf_c, hb_c, nrm_c = _filter_call(lp, nctx)
        x0_c, z_c, yc_c = _hctx_call(hu_c, lp["hy_conv_w"], lp["hy_conv_b"], hf_c, hb_c, nrm_c)
        xc = _merge_call(xc, oc_a, oc_b, x0_c, z_c, yc_c, gates_c, ctx_b(gate), r1(lp["hy_bias"]),
                         lp["w_out"].astype(BF16), r1(final_g), nctx, False)
    return x_new, xc


def _forward(x, c, ctx, c_ctx, params, final_norm_g):
    nb, seq, _ = x.shape
    nctx = ctx.shape[1]
    cm, sm = _mla_rope_lanes(*_rope_tables(seq, MLA_ROPE))
    cs, ss = _swa_rope_lanes(*_rope_tables(seq, SWA_DIM))
    ones = jnp.ones((nctx, LANES), F32)
    zeros = jnp.zeros((nctx, LANES), F32)
    lane = jnp.arange(LANES)[None, :]
    cm_c = jnp.where(lane < MLA_QK, ones, zeros)
    consts = {"rope_lat": (cm, sm, cs, ss), "rope_ctx": (cm_c, zeros, ones, zeros), "dft": _dft_tables(seq)}
    cc = jnp.concatenate([c, c_ctx[None, :], jnp.zeros((8 - nb - 1, D_MODEL), F32)], axis=0)
    xc = ctx
    depth = params["w_in"].shape[0]
    for l in range(depth):
        lp = {k: v[l] for k, v in params.items()}
        mod = _mod_call(cc, lp["mod_w"], lp["mod_b"])
        x, xc = _layer(x, xc, mod, lp, consts, l < depth - 1, l == depth - 1, final_norm_g)
    return x


def kernel(x, c, ctx, c_ctx, norm_g, mod_w, mod_b, w_in, mla_q_norm, mla_w_uq, mla_kv_norm, mla_w_ukv, swa_sink,
           hy_conv_w, hy_conv_b, hy_w1, hy_b1, hy_freq, hy_w2, hy_b2, hy_w3, hy_b3, hy_bias, w_out, final_norm_g):
    params = dict(norm_g=norm_g, mod_w=mod_w, mod_b=mod_b, w_in=w_in, mla_q_norm=mla_q_norm, mla_w_uq=mla_w_uq,
                  mla_kv_norm=mla_kv_norm, mla_w_ukv=mla_w_ukv, swa_sink=swa_sink, hy_conv_w=hy_conv_w,
                  hy_conv_b=hy_conv_b, hy_w1=hy_w1, hy_b1=hy_b1, hy_freq=hy_freq, hy_w2=hy_w2, hy_b2=hy_b2,
                  hy_w3=hy_w3, hy_b3=hy_b3, hy_bias=hy_bias, w_out=w_out)
    return _forward(x, c, ctx, c_ctx, params, final_norm_g)
```

```python
import functools
import math

import numpy as np
import jax
import jax.numpy as jnp
from jax import lax
from jax.experimental import pallas as pl
from jax.experimental.pallas import tpu as pltpu

F32 = jnp.float32
BF16 = jnp.bfloat16
HI = lax.Precision.HIGHEST

D_MODEL = 1024
DEPTH = 2
GRID_W = 64
NORM_EPS = 1e-6
ROPE_THETA = 10000.0

MLA_HEADS = 6
MLA_NOPE = 64
MLA_ROPE = 32
MLA_QK = MLA_NOPE + MLA_ROPE
MLA_V = 64
MLA_Q_RANK = 384
MLA_KV_RANK = 256
MLA_WIDTH = MLA_HEADS * MLA_V

SWA_HEADS = 6
SWA_KV_HEADS = 2
SWA_GROUP = SWA_HEADS // SWA_KV_HEADS
SWA_DIM = 64
SWA_WINDOW = 128
SWA_WIDTH = SWA_HEADS * SWA_DIM

HY_WIDTH = 256
HY_BANDS = 8
HY_EMB = 1 + 2 * HY_BANDS
HY_HIDDEN = 64
HY_DECAY_FAST = math.log(1e-2) / 0.3
HY_DECAY_SLOW = math.log(1e-2) / 1.5

D_MIX = MLA_WIDTH + SWA_WIDTH + HY_WIDTH
IN_SIZES = (MLA_Q_RANK, MLA_KV_RANK, MLA_ROPE, MLA_WIDTH, SWA_WIDTH, SWA_KV_HEADS * SWA_DIM,
            SWA_KV_HEADS * SWA_DIM, SWA_WIDTH, 3 * HY_WIDTH, HY_WIDTH)
IN_OFFS = tuple(int(v) for v in np.concatenate([[0], np.cumsum(IN_SIZES)]))

LANES = 128
FFT_N2 = 128

SEG_SIZES = (("mq", 384), ("mkv", 256), ("kr", 128), ("mg", 384), ("sq", 384), ("sk", 128),
             ("sv", 128), ("sg", 384), ("hu", 768), ("hg", 256))
SEG = {}
_o = 0
for _n, _s in SEG_SIZES:
    SEG[_n] = (_o, _o + _s)
    _o += _s
AUG_COLS = _o

SWA_ORDER = (0, 3, 1, 4, 2, 5)


def _silu(v):
    return v * jax.nn.sigmoid(v)


def _rms(v, g):
    return v * lax.rsqrt(jnp.mean(v * v, axis=-1, keepdims=True) + NORM_EPS) * g


def _swa_perm(a, axis, base):
    parts = [lax.slice_in_dim(a, base + SWA_DIM * h, base + SWA_DIM * (h + 1), axis=axis) for h in SWA_ORDER]
    return jnp.concatenate(parts, axis=axis)


def _prep_w_in(w_in):
    o = IN_OFFS
    mq, mkv, mkr, mg = (w_in[:, o[i]:o[i + 1]] for i in range(4))
    sq, sk, sv, sg, hu, hg = (w_in[:, o[i]:o[i + 1]] for i in range(4, 10))
    z = lambda n: jnp.zeros((D_MODEL, n), w_in.dtype)
    kr = jnp.concatenate([z(MLA_NOPE), mkr, z(32)], axis=1)
    w = jnp.concatenate([mq, mkv, kr, mg, _swa_perm(sq, 1, 0), sk, sv, _swa_perm(sg, 1, 0), hu, hg], axis=1)
    assert w.shape[1] == AUG_COLS
    return w.astype(BF16)


def _prep_w_out(w_out):
    return jnp.concatenate([w_out[:MLA_WIDTH], _swa_perm(w_out, 0, MLA_WIDTH), w_out[MLA_WIDTH + SWA_WIDTH:]],
                           axis=0).astype(BF16)


def _prep_w_uq(w_uq):
    zr = jnp.zeros((MLA_Q_RANK, 32), w_uq.dtype)
    cols = []
    for h in range(MLA_HEADS):
        cols += [w_uq[:, MLA_QK * h:MLA_QK * (h + 1)], zr]
    return jnp.concatenate(cols, axis=1).astype(BF16)


def _prep_w_ukv(w_ukv):
    zr = lambda n: jnp.zeros((MLA_KV_RANK, n), w_ukv.dtype)
    ks, vs = [], []
    for h in range(MLA_HEADS):
        ks += [w_ukv[:, 128 * h:128 * h + MLA_NOPE], zr(64)]
        vs += [w_ukv[:, 128 * h + MLA_NOPE:128 * (h + 1)]]
    return jnp.concatenate(ks + vs, axis=1).astype(BF16)


def _rope_tables(num_tokens, rot_dim):
    rows = num_tokens // GRID_W
    row = jnp.repeat(jnp.arange(rows, dtype=F32), GRID_W)
    col = jnp.tile(jnp.arange(GRID_W, dtype=F32), rows)
    n_freq = rot_dim // 4
    freqs = ROPE_THETA ** (-jnp.arange(n_freq, dtype=F32) / n_freq)
    ang = jnp.concatenate([row[:, None] * freqs, col[:, None] * freqs], axis=-1)
    return jnp.cos(ang), jnp.sin(ang)


def _mla_rope_lanes(cos, sin):
    n = cos.shape[0]
    z = lambda w: jnp.zeros((n, w), F32)
    c = jnp.concatenate([jnp.ones((n, MLA_NOPE), F32), cos, cos, z(32)], axis=1)
    sa = jnp.concatenate([z(MLA_NOPE), -sin, z(16), z(32)], axis=1)
    sb = jnp.concatenate([z(MLA_NOPE), z(16), sin, z(32)], axis=1)
    return c, sa, sb


def _swa_rope_lanes(cos, sin):
    z = jnp.zeros_like(sin)
    return (jnp.concatenate([cos] * 4, axis=1), jnp.concatenate([-sin, z, -sin, z], axis=1),
            jnp.concatenate([z, sin, z, sin], axis=1))


def _identity_rope_lanes(n):
    lane = jnp.arange(LANES)[None, :]
    zero = jnp.zeros((n, LANES), F32)
    return ((jnp.where(lane < MLA_QK, 1.0, 0.0) + zero, zero, zero), (zero + 1.0, zero, zero))


def _mod_kernel(c_ref, w_ref, b_ref, o_ref):
    s = _silu(c_ref[...])
    o_ref[...] = jnp.dot(s, w_ref[...], precision=HI, preferred_element_type=F32) + b_ref[...]


def _mod_call(cc, mod_w, mod_b):
    rows, n = cc.shape[0], mod_w.shape[1]
    tn = 512
    return pl.pallas_call(
        _mod_kernel,
        grid=(n // tn,),
        in_specs=[pl.BlockSpec((rows, D_MODEL), lambda j: (0, 0)),
                  pl.BlockSpec((D_MODEL, tn), lambda j: (0, j)),
                  pl.BlockSpec((1, tn), lambda j: (0, j))],
        out_specs=pl.BlockSpec((rows, tn), lambda j: (0, j)),
        out_shape=jax.ShapeDtypeStruct((rows, n), F32),
        name="mod",
    )(cc, mod_w, mod_b.reshape(1, n))


def _rope(x, tabs, sh):
    c, sa, sb = tabs
    return x * c + pltpu.roll(x, LANES - sh, 1) * sa + pltpu.roll(x, sh, 1) * sb


def _proj_kernel(x_ref, shift_ref, scale_ref, g_ref, w_ref, gq_ref, wuq_ref, gkv_ref, wukv_ref,
                 cm_ref, sam_ref, sbm_ref, cs_ref, sas_ref, sbs_ref,
                 qm_ref, km_ref, vm_ref, qs_ref, ks_ref, vs_ref, gate_ref, hu_ref):
    x = x_ref[0]
    y = _rms(x, g_ref[...])
    h = (y * (1.0 + scale_ref[0]) + shift_ref[0]).astype(BF16)

    def seg(first, last=None):
        a, b = SEG[first][0], SEG[last or first][1]
        return jnp.dot(h, w_ref[:, a:b], preferred_element_type=F32)

    tm_ = (cm_ref[...], sam_ref[...], sbm_ref[...])
    ts_ = (cs_ref[...], sas_ref[...], sbs_ref[...])
    log2e = math.log2(math.e)

    qn = _rms(seg("mq"), gq_ref[...]).astype(BF16)
    q2 = jnp.dot(qn, wuq_ref[...], preferred_element_type=F32)
    for hd in range(MLA_HEADS):
        blk = slice(LANES * hd, LANES * (hd + 1))
        qm_ref[0, :, blk] = (_rope(q2[:, blk], tm_, MLA_ROPE // 2) * (MLA_QK ** -0.5 * log2e)).astype(BF16)

    pkv = seg("mkv", "kr")
    kvn = _rms(pkv[:, :MLA_KV_RANK], gkv_ref[...]).astype(BF16)
    kv = jnp.dot(kvn, wukv_ref[...], preferred_element_type=F32)
    kr = _rope(pkv[:, MLA_KV_RANK:], tm_, MLA_ROPE // 2)
    half = MLA_HEADS * LANES
    for hd in range(MLA_HEADS):
        blk = slice(LANES * hd, LANES * (hd + 1))
        km_ref[0, :, blk] = (kv[:, blk] + kr).astype(BF16)
    vm_ref[0] = kv[:, half:].T.astype(BF16)

    sq = seg("sq")
    for j in range(3):
        blk = slice(LANES * j, LANES * (j + 1))
        qs_ref[0, :, blk] = (_rope(sq[:, blk], ts_, SWA_DIM // 2) * (SWA_DIM ** -0.5 * log2e)).astype(BF16)
    skv = seg("sk", "sv")
    ks_ref[0] = _rope(skv[:, :LANES], ts_, SWA_DIM // 2).astype(BF16)
    vs_ref[0] = skv[:, LANES:].T.astype(BF16)

    gate_ref[0, :, 0:384] = _silu(seg("mg")).astype(BF16)
    gate_ref[0, :, 384:768] = _silu(seg("sg")).astype(BF16)
    gate_ref[0, :, 768:1024] = _silu(seg("hg")).astype(BF16)
    hu_ref[0] = seg("hu")


def _proj_call(x, shift, scale, norm_g, w_aug, gq, wuq, gkv, wukv, rope_m, rope_s, tm):
    nb, n, _ = x.shape
    row = lambda b, i: (b, i, 0)
    col = lambda b, i: (b, 0, i)
    per_b = lambda b, i: (b, 0, 0)
    const = lambda b, i: (0, 0)
    tab = lambda b, i: (i, 0)
    kvw = SWA_KV_HEADS * SWA_DIM
    outs = [((n, 768), (tm, 768), row, BF16), ((n, 768), (tm, 768), row, BF16),
            ((MLA_WIDTH, n), (MLA_WIDTH, tm), col, BF16),
            ((n, SWA_WIDTH), (tm, SWA_WIDTH), row, BF16), ((n, kvw), (tm, kvw), row, BF16),
            ((kvw, n), (kvw, tm), col, BF16),
            ((n, D_MODEL), (tm, D_MODEL), row, BF16), ((n, 3 * HY_WIDTH), (tm, 3 * HY_WIDTH), row, F32)]
    return pl.pallas_call(
        _proj_kernel,
        grid=(nb, n // tm),
        in_specs=[pl.BlockSpec((1, tm, D_MODEL), row),
                  pl.BlockSpec((1, 1, D_MODEL), per_b), pl.BlockSpec((1, 1, D_MODEL), per_b),
                  pl.BlockSpec((1, D_MODEL), const),
                  pl.BlockSpec((D_MODEL, AUG_COLS), const),
                  pl.BlockSpec((1, MLA_Q_RANK), const), pl.BlockSpec(wuq.shape, const),
                  pl.BlockSpec((1, MLA_KV_RANK), const), pl.BlockSpec(wukv.shape, const)]
                 + [pl.BlockSpec((tm, LANES), tab)] * 6,
        out_specs=[pl.BlockSpec((1,) + blk, imap) for _, blk, imap, _ in outs],
        out_shape=[jax.ShapeDtypeStruct((nb,) + shp, dt) for shp, _, _, dt in outs],
        compiler_params=pltpu.CompilerParams(dimension_semantics=("parallel", "parallel")),
        name="proj",
    )(x, shift, scale, norm_g, w_aug, gq, wuq, gkv, wukv, *rope_m, *rope_s)


def _sublane_all(v, op):
    for sh in (4, 2, 1):
        v = op(v, pltpu.roll(v, sh, 0))
    return v


def _mla_kernel(*refs, n_lat, tk, has_lat, unroll):
    if has_lat:
        q_ref, kl_ref, vtl_ref, kc_ref, vtc_ref, o_ref, s_sc, mx_sc, m_sc, l_sc, acc_sc = refs
    else:
        q_ref, kc_ref, vtc_ref, o_ref, s_sc, mx_sc, m_sc, l_sc, acc_sc = refs
    tq = q_ref.shape[1]
    nc = kc_ref.shape[1]
    nt = (((1,), (1,)), ((), ()))
    heads = range(2)
    m_sc[...] = jnp.full(m_sc.shape, -1e30, F32)
    l_sc[...] = jnp.zeros(l_sc.shape, F32)
    acc_sc[...] = jnp.zeros(acc_sc.shape, F32)

    def scores(slot, hh, k):
        n = k.shape[0]
        q = q_ref[0, :, LANES * hh:LANES * (hh + 1)]
        s = lax.dot_general(k, q, nt, preferred_element_type=F32)
        s_sc[slot, hh, 0:n, :] = s
        mx_sc[slot, hh] = jnp.max(s.reshape(n // 8, 8, tq), axis=0)

    def update(slot, hh, vt):
        n = vt.shape[1]
        m_prev = m_sc[hh]
        m_new = jnp.maximum(m_prev, _sublane_all(mx_sc[slot, hh], jnp.maximum))
        alpha = jnp.exp2(m_prev - m_new)
        p3 = jnp.exp2(s_sc[slot, hh, 0:n, :].reshape(n // 8, 8, tq) - m_new[None])
        l_sc[hh] = alpha * l_sc[hh] + jnp.sum(p3, axis=0)
        pv = jnp.dot(vt, p3.reshape(n, tq).astype(BF16), preferred_element_type=F32)
        acc_sc[hh] = acc_sc[hh] * jnp.tile(alpha, (LANES // 8, 1)) + pv
        m_sc[hh] = m_new

    def k_lat(i, hh):
        return kl_ref[0, pl.ds(pl.multiple_of(i * tk, tk), tk), LANES * hh:LANES * (hh + 1)]

    def vt_lat(i):
        return vtl_ref[0, :, pl.ds(pl.multiple_of(i * tk, tk), tk)]

    def k_ctx(hh):
        return kc_ref[0, :, LANES * hh:LANES * (hh + 1)]

    n = n_lat // tk if has_lat else 0
    if n == 0:
        for hh in heads:
            scores(0, hh, k_ctx(hh))
    else:
        for hh in heads:
            scores(0, hh, k_lat(0, hh))
        first = ((n - 1) // unroll) * unroll if unroll % 2 == 0 else 0
        if first:
            def body(j, carry):
                for t in range(unroll):
                    i = j * unroll + t
                    for hh in heads:
                        scores((t + 1) % 2, hh, k_lat(i + 1, hh))
                    for hh in heads:
                        update(t % 2, hh, vt_lat(i))
                return carry
            lax.fori_loop(0, first // unroll, body, 0)
        for i in range(first, n):
            for hh in heads:
                if i + 1 < n:
                    scores((i + 1) % 2, hh, k_lat(i + 1, hh))
                else:
                    scores((i + 1) % 2, hh, k_ctx(hh))
            for hh in heads:
                update(i % 2, hh, vt_lat(i))
    for hh in heads:
        update(n % 2, hh, vtc_ref[0])
    outs = []
    for hh in range(2):
        inv = 1.0 / _sublane_all(l_sc[hh], jnp.add)
        outs.append(acc_sc[hh] * jnp.tile(inv, (LANES // 8, 1)))
    row = lax.broadcasted_iota(jnp.int32, (LANES, tq), 0)
    o_ref[0] = jnp.where(row < MLA_V, outs[0], outs[1]).T


def _mla_call(q, k_lat, vt_lat, k_ctx, vt_ctx, tq, tk, unroll=2):
    nb, nq, _ = q.shape
    nc = k_ctx.shape[1]
    has_lat = k_lat is not None
    qmap = lambda b, p, i: (b, i, p)
    kvmap = lambda b, p, i: (b, 0, p)
    vtmap = lambda b, p, i: (b, p, 0)
    in_specs = [pl.BlockSpec((1, tq, 2 * LANES), qmap)]
    args = [q]
    n_lat = 0
    if has_lat:
        n_lat = k_lat.shape[1]
        in_specs += [pl.BlockSpec((1, n_lat, 2 * LANES), kvmap), pl.BlockSpec((1, LANES, n_lat), vtmap)]
        args += [k_lat, vt_lat]
    in_specs += [pl.BlockSpec((1, nc, 2 * LANES), kvmap), pl.BlockSpec((1, LANES, nc), vtmap)]
    args += [k_ctx, vt_ctx]
    return pl.pallas_call(
        functools.partial(_mla_kernel, n_lat=n_lat, tk=tk, has_lat=has_lat, unroll=unroll),
        grid=(nb, MLA_HEADS // 2, nq // tq),
        in_specs=in_specs,
        out_specs=pl.BlockSpec((1, tq, LANES), qmap),
        out_shape=jax.ShapeDtypeStruct((nb, nq, MLA_WIDTH), F32),
        scratch_shapes=[pltpu.VMEM((2, 2, max(tk, nc), tq), F32), pltpu.VMEM((2, 2, 8, tq), F32),
                        pltpu.VMEM((2, 8, tq), F32), pltpu.VMEM((2, 8, tq), F32), pltpu.VMEM((2, LANES, tq), F32)],
        compiler_params=pltpu.CompilerParams(dimension_semantics=("parallel", "parallel", "parallel")),
        name="mla_lat" if has_lat else "mla_ctx",
    )(*args)


def _swa_kernel(*refs, seq, band):
    if band:
        sink_ref, q_ref, kp_ref, kc_ref, kn_ref, vp_ref, vc_ref, vn_ref, kx_ref, vx_ref, o_ref, s_sc, mx_sc = refs
    else:
        sink_ref, q_ref, kx_ref, vx_ref, o_ref, s_sc, mx_sc = refs
    tq = q_ref.shape[1]
    nt = (((1,), (1,)), ((), ()))
    lane = lax.broadcasted_iota(jnp.int32, (tq, LANES), 1)
    row = lax.broadcasted_iota(jnp.int32, (LANES, tq), 0)
    keys, vals = kx_ref[0], vx_ref[0]
    nk = keys.shape[0]
    if band:
        i = pl.program_id(1)
        keys = jnp.concatenate([keys, kp_ref[0], kc_ref[0], kn_ref[0]], axis=0)
        vals = jnp.concatenate([vals, vp_ref[0], vc_ref[0], vn_ref[0]], axis=1)
        nb = tq + 2 * SWA_WINDOW
        kpos = i * tq - SWA_WINDOW + lax.broadcasted_iota(jnp.int32, (nb, tq), 0)
        qpos = i * tq + lax.broadcasted_iota(jnp.int32, (nb, tq), 1)
        valid = (jnp.abs(kpos - qpos) <= SWA_WINDOW) & (kpos >= 0) & (kpos < seq)

    for j in range(SWA_HEADS // 2):
        qp = q_ref[0, :, LANES * j:LANES * (j + 1)]
        for g in range(SWA_KV_HEADS):
            q = jnp.where((lane >= SWA_DIM * g) & (lane < SWA_DIM * (g + 1)), qp, jnp.zeros_like(qp))
            s = lax.dot_general(keys, q, nt, preferred_element_type=F32)
            if band:
                s = jnp.concatenate([s[:nk], jnp.where(valid, s[nk:], -1e30)], axis=0)
            s_sc[2 * j + g] = s
            mx_sc[2 * j + g] = jnp.max(s.reshape(-1, 8, tq), axis=0)
    for j in range(SWA_HEADS // 2):
        res = []
        for g in range(SWA_KV_HEADS):
            sink = sink_ref[SWA_ORDER[2 * j + g]] * math.log2(math.e)
            m = jnp.maximum(_sublane_all(mx_sc[2 * j + g], jnp.maximum), sink)
            p3 = jnp.exp2(s_sc[2 * j + g].reshape(-1, 8, tq) - m[None])
            denom = _sublane_all(jnp.sum(p3, axis=0), jnp.add) + jnp.exp2(sink - m)
            o = jnp.dot(vals, p3.reshape(-1, tq).astype(BF16), preferred_element_type=F32)
            res.append(o * jnp.tile(1.0 / denom, (LANES // 8, 1)))
        o_ref[0, :, LANES * j:LANES * (j + 1)] = jnp.where(row < SWA_DIM, res[0], res[1]).T


def _swa_call(sink, q, k, vt, k_ctx, vt_ctx, band):
    nb, nq, _ = q.shape
    nc = k_ctx.shape[1]
    tq = min(nq, 256) if band else nq
    nblk = nq // tq
    nkeys = nc + (tq + 2 * SWA_WINDOW if band else 0)
    wb = tq // SWA_WINDOW
    nwb = nq // SWA_WINDOW
    kw = SWA_KV_HEADS * SWA_DIM
    cur = lambda b, i: (b, i, 0)
    ctx = lambda b, i: (b, 0, 0)
    in_specs = [pl.BlockSpec(memory_space=pltpu.SMEM), pl.BlockSpec((1, tq, SWA_WIDTH), cur)]
    args = [sink, q]
    if band:
        in_specs += [pl.BlockSpec((1, SWA_WINDOW, kw), lambda b, i: (b, jnp.maximum(i * wb - 1, 0), 0)),
                     pl.BlockSpec((1, tq, kw), cur),
                     pl.BlockSpec((1, SWA_WINDOW, kw), lambda b, i: (b, jnp.minimum((i + 1) * wb, nwb - 1), 0)),
                     pl.BlockSpec((1, kw, SWA_WINDOW), lambda b, i: (b, 0, jnp.maximum(i * wb - 1, 0))),
                     pl.BlockSpec((1, kw, tq), lambda b, i: (b, 0, i)),
                     pl.BlockSpec((1, kw, SWA_WINDOW), lambda b, i: (b, 0, jnp.minimum((i + 1) * wb, nwb - 1)))]
        args += [k, k, k, vt, vt, vt]
    in_specs += [pl.BlockSpec((1, nc, kw), ctx), pl.BlockSpec((1, kw, nc), ctx)]
    args += [k_ctx, vt_ctx]
    return pl.pallas_call(
        functools.partial(_swa_kernel, seq=nq, band=band),
        grid=(nb, nblk),
        in_specs=in_specs,
        out_specs=pl.BlockSpec((1, tq, SWA_WIDTH), cur),
        out_shape=jax.ShapeDtypeStruct((nb, nq, SWA_WIDTH), F32),
        scratch_shapes=[pltpu.VMEM((SWA_HEADS, nkeys, tq), F32), pltpu.VMEM((SWA_HEADS, 8, tq), F32)],
        compiler_params=pltpu.CompilerParams(dimension_semantics=("parallel", "parallel")),
        name="swa_lat" if band else "swa_ctx",
    )(*args)


def _filter_kernel(z_ref, w1_ref, b1_ref, fr_ref, w2_ref, b2_ref, w3_ref, b3_ref, dl_ref,
                   hf_ref, hb_ref, nrm_ref, *, num_tokens):
    i = pl.program_id(0)
    tl = z_ref.shape[0]
    fr = fr_ref[...]
    h = jnp.sin(fr * (jnp.dot(z_ref[...], w1_ref[...], precision=HI, preferred_element_type=F32) + b1_ref[...]))
    h = jnp.sin(fr * (jnp.dot(h, w2_ref[...], precision=HI, preferred_element_type=F32) + b2_ref[...]))
    h = jnp.dot(h, w3_ref[...], precision=HI, preferred_element_type=F32) + b3_ref[...]
    row = i * tl + lax.broadcasted_iota(jnp.int32, (tl, HY_WIDTH), 0)
    t = row.astype(F32) * (1.0 / (num_tokens - 1))
    decay = jnp.exp(-t * dl_ref[...])
    hf = h[:, :HY_WIDTH] * decay
    hb = jnp.where(row == 0, 0.0, h[:, HY_WIDTH:] * decay)
    hf_ref[...] = hf
    hb_ref[...] = hb
    part = jnp.sum(jnp.abs(hf) + jnp.abs(hb), axis=0, keepdims=True)

    @pl.when(i == 0)
    def _():
        nrm_ref[...] = jnp.zeros(nrm_ref.shape, F32)

    nrm_ref[...] += part


def _filter_call(lp, num_tokens):
    tl = min(num_tokens, 1024)
    t = jnp.linspace(0.0, 1.0, num_tokens, dtype=F32)[:, None]
    w = (2.0 * math.pi / num_tokens) * jnp.arange(num_tokens, dtype=F32)[:, None]
    bands = jnp.linspace(1e-4, HY_BANDS - 1, HY_BANDS, dtype=F32)[None, :]
    z = jnp.concatenate([t, jnp.cos(bands * w), -jnp.sin(bands * w),
                         jnp.zeros((num_tokens, LANES - HY_EMB), F32)], axis=-1)
    w1 = jnp.concatenate([lp["hy_w1"], jnp.zeros((LANES - HY_EMB, HY_HIDDEN), F32)], axis=0)
    deltas = jnp.abs(jnp.linspace(HY_DECAY_FAST, HY_DECAY_SLOW, HY_WIDTH, dtype=F32)).reshape(1, HY_WIDTH)
    const = lambda i: (0, 0)
    rowm = lambda i: (i, 0)
    r1 = lambda a: a.reshape(1, -1)
    return pl.pallas_call(
        functools.partial(_filter_kernel, num_tokens=num_tokens),
        grid=(num_tokens // tl,),
        in_specs=[pl.BlockSpec((tl, LANES), rowm),
                  pl.BlockSpec((LANES, HY_HIDDEN), const), pl.BlockSpec((1, HY_HIDDEN), const),
                  pl.BlockSpec((1, HY_HIDDEN), const),
                  pl.BlockSpec((HY_HIDDEN, HY_HIDDEN), const), pl.BlockSpec((1, HY_HIDDEN), const),
                  pl.BlockSpec((HY_HIDDEN, 2 * HY_WIDTH), const), pl.BlockSpec((1, 2 * HY_WIDTH), const),
                  pl.BlockSpec((1, HY_WIDTH), const)],
        out_specs=[pl.BlockSpec((tl, HY_WIDTH), rowm), pl.BlockSpec((tl, HY_WIDTH), rowm),
                   pl.BlockSpec((1, HY_WIDTH), const)],
        out_shape=[jax.ShapeDtypeStruct((num_tokens, HY_WIDTH), F32)] * 2 + [jax.ShapeDtypeStruct((1, HY_WIDTH), F32)],
        compiler_params=pltpu.CompilerParams(dimension_semantics=("arbitrary",)),
        name="hy_filter",
    )(z, w1, r1(lp["hy_b1"]), r1(lp["hy_freq"]), lp["hy_w2"], r1(lp["hy_b2"]), lp["hy_w3"], r1(lp["hy_b3"]), deltas)


def _short_conv(u, prev_row, next_row, w_ref, b_ref):
    n = u.shape[0]
    row = lax.broadcasted_iota(jnp.int32, u.shape, 0)
    up = jnp.where(row == 0, prev_row, pltpu.roll(u, 1, 0))
    un = jnp.where(row == n - 1, next_row, pltpu.roll(u, n - 1, 0))
    return up * w_ref[0:1, :] + u * w_ref[1:2, :] + un * w_ref[2:3, :] + b_ref[...]


def _hconv_kernel(u_ref, up_ref, un_ref, w_ref, b_ref, x0_ref, z_ref):
    i = pl.program_id(1)
    last = pl.num_programs(1) - 1
    prev_row = jnp.where(i == 0, 0.0, up_ref[0, 7:8, :])
    next_row = jnp.where(i == last, 0.0, un_ref[0, 0:1, :])
    uc = _short_conv(u_ref[0], prev_row, next_row, w_ref, b_ref)
    x0_ref[0] = uc[:, :HY_WIDTH]
    z_ref[0] = uc[:, 2 * HY_WIDTH:] * uc[:, HY_WIDTH:2 * HY_WIDTH]


def _hconv_call(hu, conv_w, conv_b, tl):
    nb, n, w3 = hu.shape
    hb = tl // 8
    nh = n // 8
    cur = lambda b, i: (b, i, 0)
    prev = lambda b, i: (b, jnp.maximum(i * hb - 1, 0), 0)
    nxt = lambda b, i: (b, jnp.minimum((i + 1) * hb, nh - 1), 0)
    const = lambda b, i: (0, 0)
    return pl.pallas_call(
        _hconv_kernel,
        grid=(nb, n // tl),
        in_specs=[pl.BlockSpec((1, tl, w3), cur), pl.BlockSpec((1, 8, w3), prev), pl.BlockSpec((1, 8, w3), nxt),
                  pl.BlockSpec((3, w3), const), pl.BlockSpec((1, w3), const)],
        out_specs=[pl.BlockSpec((1, tl, HY_WIDTH), cur)] * 2,
        out_shape=[jax.ShapeDtypeStruct((nb, n, HY_WIDTH), F32)] * 2,
        compiler_params=pltpu.CompilerParams(dimension_semantics=("parallel", "parallel")),
        name="hy_conv",
    )(hu, hu, hu, conv_w, conv_b.reshape(1, w3))


def _dft_tables(seq):
    n1 = 2 * seq // FFT_N2
    nd = seq // FFT_N2
    n = n1 * FFT_N2
    k1 = np.arange(n1)[:, None]
    a = 2.0 * np.pi * ((k1 * np.arange(nd)[None, :]) % n1) / n1
    c, s = np.cos(a), np.sin(a)
    fwd1 = np.block([[c, s], [-s, c]])
    inv1 = np.block([[c.T, -s.T], [s.T, c.T]])
    k2 = np.arange(FFT_N2)[None, :, None]
    n2 = np.arange(FFT_N2)[None, None, :]
    g = 2.0 * np.pi * ((n2 * (k1[:, :, None] + n1 * k2)) % n) / n
    gr, gi = np.cos(g), -np.sin(g)
    cast = lambda t: jnp.asarray(np.ascontiguousarray(t, dtype=np.float32), BF16)
    return (cast(fwd1), cast(inv1), cast(gr), cast(gi),
            cast(np.transpose(gr, (0, 2, 1))), cast(np.transpose(-gi, (0, 2, 1))))


def _stage1_kernel(m_ref, z_ref, a_ref):
    zs = jnp.concatenate([z_ref[0, 0], z_ref[0, 1]], axis=0).astype(BF16)
    a_ref[0] = jnp.dot(m_ref[...], zs, preferred_element_type=F32).astype(a_ref.dtype)


def _stage1_call(mat, z, out_dtype, tn, name):
    npair, _, rows, cols = z.shape
    mo = mat.shape[0]
    return pl.pallas_call(
        _stage1_kernel,
        grid=(npair, cols // tn),
        in_specs=[pl.BlockSpec(mat.shape, lambda p, j: (0, 0)),
                  pl.BlockSpec((1, 2, rows, tn), lambda p, j: (p, 0, 0, j))],
        out_specs=pl.BlockSpec((1, mo, tn), lambda p, j: (p, 0, j)),
        out_shape=jax.ShapeDtypeStruct((npair, mo, cols), out_dtype),
        compiler_params=pltpu.CompilerParams(dimension_semantics=("parallel", "parallel")),
        name=name,
    )(mat, z)


def _cplx_mat(re, im):
    return jnp.concatenate([jnp.concatenate([re, -im], axis=1), jnp.concatenate([im, re], axis=1)], axis=0)


def _spec_kernel(gr_ref, gi_ref, a_ref, nrm_ref, k_ref, *, kb, scale):
    inv = scale / (nrm_ref[...] + NORM_EPS)
    for j in range(kb):
        g2 = _cplx_mat(gr_ref[j], gi_ref[j])
        x1 = jnp.dot(g2, jnp.concatenate([a_ref[0, 0, j], a_ref[0, 1, j]], axis=0), preferred_element_type=F32)
        x2 = jnp.dot(g2, jnp.concatenate([a_ref[1, 0, j], a_ref[1, 1, j]], axis=0), preferred_element_type=F32)
        k_ref[j, :FFT_N2, :] = (x1[:FFT_N2] + x2[:FFT_N2]) * inv
        k_ref[j, FFT_N2:, :] = (x1[FFT_N2:] - x2[FFT_N2:]) * inv


def _spec_call(gr, gi, a, nrm, kb, scale):
    n1 = gr.shape[0]
    tab = lambda i: (i, 0, 0)
    return pl.pallas_call(
        functools.partial(_spec_kernel, kb=kb, scale=scale),
        grid=(n1 // kb,),
        in_specs=[pl.BlockSpec((kb, FFT_N2, FFT_N2), tab), pl.BlockSpec((kb, FFT_N2, FFT_N2), tab),
                  pl.BlockSpec((2, 2, kb, FFT_N2, HY_WIDTH), lambda i: (0, 0, i, 0, 0)),
                  pl.BlockSpec((1, HY_WIDTH), lambda i: (0, 0))],
        out_specs=pl.BlockSpec((kb, 2 * FFT_N2, HY_WIDTH), tab),
        out_shape=jax.ShapeDtypeStruct((n1, 2 * FFT_N2, HY_WIDTH), F32),
        compiler_params=pltpu.CompilerParams(dimension_semantics=("parallel",)),
        name="hy_spec",
    )(gr, gi, a, nrm)


def _stage2_kernel(gr_ref, gi_ref, hr_ref, hi_ref, k_ref, a_ref, b_ref, *, kb):
    for j in range(kb):
        s = jnp.concatenate([a_ref[0, 0, j], a_ref[0, 1, j]], axis=0)
        x = jnp.dot(_cplx_mat(gr_ref[j], gi_ref[j]), s, preferred_element_type=F32)
        xr, xi = x[:FFT_N2], x[FFT_N2:]
        kr, ki = k_ref[j, :FFT_N2, :], k_ref[j, FFT_N2:, :]
        y = jnp.concatenate([xr * kr - xi * ki, xr * ki + xi * kr], axis=0).astype(BF16)
        b = jnp.dot(_cplx_mat(hr_ref[j], hi_ref[j]), y, preferred_element_type=F32)
        b_ref[0, 0, j] = b[:FFT_N2].astype(b_ref.dtype)
        b_ref[0, 1, j] = b[FFT_N2:].astype(b_ref.dtype)


def _stage2_call(gr, gi, hr, hi, kf, a, kb):
    npair = a.shape[0]
    n1 = gr.shape[0]
    tab = lambda p, i: (i, 0, 0)
    dat = lambda p, i: (p, 0, i, 0, 0)
    return pl.pallas_call(
        functools.partial(_stage2_kernel, kb=kb),
        grid=(npair, n1 // kb),
        in_specs=[pl.BlockSpec((kb, FFT_N2, FFT_N2), tab)] * 4
                 + [pl.BlockSpec((kb, 2 * FFT_N2, HY_WIDTH), tab),
                    pl.BlockSpec((1, 2, kb, FFT_N2, HY_WIDTH), dat)],
        out_specs=pl.BlockSpec((1, 2, kb, FFT_N2, HY_WIDTH), dat),
        out_shape=jax.ShapeDtypeStruct(a.shape, BF16),
        compiler_params=pltpu.CompilerParams(dimension_semantics=("parallel", "parallel")),
        name="hy_stage2",
    )(gr, gi, hr, hi, kf, a)


def _long_conv(z, hf, hb, nrm, tables):
    nb, seq, w = z.shape
    fwd1, inv1, gr, gi, hr, hi = tables
    n1 = 2 * seq // FFT_N2
    nd = seq // FFT_N2
    cols = FFT_N2 * w
    tn = min(cols, 4096)
    kb = min(n1, 8)
    zero = jnp.zeros_like(hf)
    zf = jnp.stack([jnp.stack([hf, zero]), jnp.stack([hb, zero])]).reshape(2, 2, nd, cols)
    af = _stage1_call(fwd1, zf, BF16, tn, "hy_fstage1").reshape(2, 2, n1, FFT_N2, w)
    kf = _spec_call(gr, gi, af, nrm, kb, 1.0 / (n1 * FFT_N2))
    zd = z.reshape(nb // 2, 2, nd, cols)
    a = _stage1_call(fwd1, zd, BF16, tn, "hy_stage1").reshape(nb // 2, 2, n1, FFT_N2, w)
    b = _stage2_call(gr, gi, hr, hi, kf, a, kb).reshape(nb // 2, 2, n1, cols)
    y = _stage1_call(inv1, b, F32, tn, "hy_istage1")
    return y.reshape(nb, seq, w)


def _hctx_kernel(u_ref, w_ref, b_ref, hf_ref, hb_ref, nrm_ref, c_ref, s_ref, ct_ref, st_ref,
                 x0_ref, z_ref, y_ref):
    u = u_ref[0]
    n = u.shape[0]
    zero_row = jnp.zeros((1, u.shape[1]), F32)
    uc = _short_conv(u, zero_row, zero_row, w_ref, b_ref)
    x0 = uc[:, :HY_WIDTH]
    z = uc[:, 2 * HY_WIDTH:] * uc[:, HY_WIDTH:2 * HY_WIDTH]
    dot = lambda a, b: jnp.dot(a, b, precision=HI, preferred_element_type=F32)
    c, s = c_ref[...], s_ref[...]
    inv = (1.0 / (2 * n)) / (nrm_ref[...] + NORM_EPS)
    kr = (dot(c, hf_ref[...]) + dot(c, hb_ref[...])) * inv
    ki = (dot(s, hb_ref[...]) - dot(s, hf_ref[...])) * inv
    xr, xi = dot(c, z), -dot(s, z)
    yr, yi = xr * kr - xi * ki, xr * ki + xi * kr
    x0_ref[0] = x0
    z_ref[0] = z
    y_ref[0] = dot(ct_ref[...], yr) - dot(st_ref[...], yi)


def _hctx_call(hu, conv_w, conv_b, hf, hb, nrm):
    nb, n, w3 = hu.shape
    k = np.arange(2 * n)[:, None]
    a = 2.0 * np.pi * ((k * np.arange(n)[None, :]) % (2 * n)) / (2 * n)
    c, s = np.cos(a).astype(np.float32), np.sin(a).astype(np.float32)
    const = lambda b: (0, 0)
    cur = lambda b: (b, 0, 0)
    out = pl.BlockSpec((1, n, HY_WIDTH), cur)
    return pl.pallas_call(
        _hctx_kernel,
        grid=(nb,),
        in_specs=[pl.BlockSpec((1, n, w3), cur), pl.BlockSpec((3, w3), const), pl.BlockSpec((1, w3), const),
                  pl.BlockSpec((n, HY_WIDTH), const), pl.BlockSpec((n, HY_WIDTH), const),
                  pl.BlockSpec((1, HY_WIDTH), const),
                  pl.BlockSpec((2 * n, n), const), pl.BlockSpec((2 * n, n), const),
                  pl.BlockSpec((n, 2 * n), const), pl.BlockSpec((n, 2 * n), const)],
        out_specs=[out, out, out],
        out_shape=[jax.ShapeDtypeStruct((nb, n, HY_WIDTH), F32)] * 3,
        compiler_params=pltpu.CompilerParams(dimension_semantics=("parallel",)),
        name="hy_ctx",
    )(hu, conv_w, conv_b.reshape(1, w3), hf, hb, nrm, jnp.asarray(c), jnp.asarray(s),
      jnp.asarray(c.T.copy()), jnp.asarray(s.T.copy()))


def _merge_kernel(x_ref, oa_ref, ob_ref, x0_ref, z_ref, yc_ref, gate_ref, gx_ref, bias_ref, w_ref, fg_ref,
                  o_ref, *, final):
    g = gate_ref[0].astype(F32)
    ya = oa_ref[0] * g[:, 0:384]
    yb = ob_ref[0] * g[:, 384:768]
    z = z_ref[0]
    yh = x0_ref[0] * (yc_ref[0] + bias_ref[...] * z) * g[:, 768:1024]
    y = jnp.concatenate([ya, yb, yh], axis=1).astype(BF16)
    xn = x_ref[0] + gx_ref[0] * jnp.dot(y, w_ref[...], preferred_element_type=F32)
    if final:
        xn = _rms(xn, fg_ref[...])
    o_ref[0] = xn


def _merge_call(x, oa, ob, x0, z, yc, gates, gate_x, bias, w_out, fg, tm, final):
    nb, n, _ = x.shape
    row = lambda b, i: (b, i, 0)
    per_b = lambda b, i: (b, 0, 0)
    const = lambda b, i: (0, 0)
    return pl.pallas_call(
        functools.partial(_merge_kernel, final=final),
        grid=(nb, n // tm),
        in_specs=[pl.BlockSpec((1, tm, D_MODEL), row),
                  pl.BlockSpec((1, tm, MLA_WIDTH), row), pl.BlockSpec((1, tm, SWA_WIDTH), row),
                  pl.BlockSpec((1, tm, HY_WIDTH), row), pl.BlockSpec((1, tm, HY_WIDTH), row),
                  pl.BlockSpec((1, tm, HY_WIDTH), row),
                  pl.BlockSpec((1, tm, D_MODEL), row), pl.BlockSpec((1, 1, D_MODEL), per_b),
                  pl.BlockSpec((1, HY_WIDTH), const), pl.BlockSpec((D_MIX, D_MODEL), const),
                  pl.BlockSpec((1, D_MODEL), const)],
        out_specs=pl.BlockSpec((1, tm, D_MODEL), row),
        out_shape=jax.ShapeDtypeStruct((nb, n, D_MODEL), F32),
        compiler_params=pltpu.CompilerParams(dimension_semantics=("parallel", "parallel")),
        name="merge",
    )(x, oa, ob, x0, z, yc, gates, gate_x, bias, w_out, fg)


def _layer(x, xc, mod, lp, consts, update_ctx, final, final_g):
    nb, seq, _ = x.shape
    nctx = xc.shape[1]
    w_aug = _prep_w_in(lp["w_in"])
    w_out = _prep_w_out(lp["w_out"])
    wuq = _prep_w_uq(lp["mla_w_uq"])
    wukv = _prep_w_ukv(lp["mla_w_ukv"])
    r1 = lambda a: a.reshape(1, -1)
    shift, scale, gate = (mod[:, D_MODEL * j:D_MODEL * (j + 1)] for j in range(3))
    sel = lambda a, lo, hi: a[lo:hi].reshape(hi - lo, 1, D_MODEL)
    ctx_b = lambda a: jnp.broadcast_to(a[nb:nb + 1].reshape(1, 1, D_MODEL), (nb, 1, D_MODEL))

    tm = min(seq, 512)
    px = _proj_call(x, sel(shift, 0, nb), sel(scale, 0, nb), r1(lp["norm_g"]), w_aug, r1(lp["mla_q_norm"]), wuq,
                    r1(lp["mla_kv_norm"]), wukv, *consts["rope_lat"], tm)
    pc = _proj_call(xc, ctx_b(shift), ctx_b(scale), r1(lp["norm_g"]), w_aug, r1(lp["mla_q_norm"]), wuq,
                    r1(lp["mla_kv_norm"]), wukv, *consts["rope_ctx"], nctx)
    qm, km, vm, qs, ks, vs, gates, hu = px
    qm_c, km_c, vm_c, qs_c, ks_c, vs_c, gates_c, hu_c = pc

    o_a = _mla_call(qm, km, vm, km_c, vm_c, min(seq, 512), 1024 if seq >= 2048 else 128, 2)
    o_b = _swa_call(lp["swa_sink"], qs, ks, vs, ks_c, vs_c, True)

    hf, hb, nrm = _filter_call(lp, seq)
    x0, z = _hconv_call(hu, lp["hy_conv_w"], lp["hy_conv_b"], min(seq, 1024))
    yc = _long_conv(z, hf, hb, nrm, consts["dft"])

    x_new = _merge_call(x, o_a, o_b, x0, z, yc, gates, sel(gate, 0, nb), r1(lp["hy_bias"]),
                        w_out, r1(final_g), tm, final)
    if update_ctx:
        oc_a = _mla_call(qm_c, None, None, km_c, vm_c, nctx, nctx)
        oc_b = _swa_call(lp["swa_sink"], qs_c, None, None, ks_c, vs_c, False)
        hf_c, hb_c, nrm_c = _filter_call(lp, nctx)
        x0_c, z_c, yc_c = _hctx_call(hu_c, lp["hy_conv_w"], lp["hy_conv_b"], hf_c, hb_c, nrm_c)
        xc = _merge_call(xc, oc_a, oc_b, x0_c, z_c, yc_c, gates_c, ctx_b(gate), r1(lp["hy_bias"]),
                         w_out, r1(final_g), nctx, False)
    return x_new, xc


def _forward(x, c, ctx, c_ctx, params, final_norm_g):
    nb, seq, _ = x.shape
    nctx = ctx.shape[1]
    consts = {"rope_lat": (_mla_rope_lanes(*_rope_tables(seq, MLA_ROPE)), _swa_rope_lanes(*_rope_tables(seq, SWA_DIM))),
              "rope_ctx": _identity_rope_lanes(nctx), "dft": _dft_tables(seq)}
    cc = jnp.concatenate([c, c_ctx[None, :], jnp.zeros((8 - nb - 1, D_MODEL), F32)], axis=0)
    xc = ctx
    depth = params["w_in"].shape[0]
    for l in range(depth):
        lp = {k: v[l] for k, v in params.items()}
        mod = _mod_call(cc, lp["mod_w"], lp["mod_b"])
        x, xc = _layer(x, xc, mod, lp, consts, l < depth - 1, l == depth - 1, final_norm_g)
    return x


def kernel(x, c, ctx, c_ctx, norm_g, mod_w, mod_b, w_in, mla_q_norm, mla_w_uq, mla_kv_norm, mla_w_ukv, swa_sink,
           hy_conv_w, hy_conv_b, hy_w1, hy_b1, hy_freq, hy_w2, hy_b2, hy_w3, hy_b3, hy_bias, w_out, final_norm_g):
    params = dict(norm_g=norm_g, mod_w=mod_w, mod_b=mod_b, w_in=w_in, mla_q_norm=mla_q_norm, mla_w_uq=mla_w_uq,
                  mla_kv_norm=mla_kv_norm, mla_w_ukv=mla_w_ukv, swa_sink=swa_sink, hy_conv_w=hy_conv_w,
                  hy_conv_b=hy_conv_b, hy_w1=hy_w1, hy_b1=hy_b1, hy_freq=hy_freq, hy_w2=hy_w2, hy_b2=hy_b2,
                  hy_w3=hy_w3, hy_b3=hy_b3, hy_bias=hy_bias, w_out=w_out)
    return _forward(x, c, ctx, c_ctx, params, final_norm_g)
```

```python
import functools
import math

import numpy as np
import jax
import jax.numpy as jnp
from jax import lax
from jax.experimental import pallas as pl
from jax.experimental.pallas import tpu as pltpu

F32 = jnp.float32
BF16 = jnp.bfloat16
HI = lax.Precision.HIGHEST

D_MODEL = 1024
DEPTH = 2
GRID_W = 64
NORM_EPS = 1e-6
ROPE_THETA = 10000.0

MLA_HEADS = 6
MLA_NOPE = 64
MLA_ROPE = 32
MLA_QK = MLA_NOPE + MLA_ROPE
MLA_V = 64
MLA_Q_RANK = 384
MLA_KV_RANK = 256
MLA_WIDTH = MLA_HEADS * MLA_V

SWA_HEADS = 6
SWA_KV_HEADS = 2
SWA_GROUP = SWA_HEADS // SWA_KV_HEADS
SWA_DIM = 64
SWA_WINDOW = 128
SWA_WIDTH = SWA_HEADS * SWA_DIM

HY_WIDTH = 256
HY_BANDS = 8
HY_EMB = 1 + 2 * HY_BANDS
HY_HIDDEN = 64
HY_DECAY_FAST = math.log(1e-2) / 0.3
HY_DECAY_SLOW = math.log(1e-2) / 1.5

D_MIX = MLA_WIDTH + SWA_WIDTH + HY_WIDTH
IN_SIZES = (MLA_Q_RANK, MLA_KV_RANK, MLA_ROPE, MLA_WIDTH, SWA_WIDTH, SWA_KV_HEADS * SWA_DIM,
            SWA_KV_HEADS * SWA_DIM, SWA_WIDTH, 3 * HY_WIDTH, HY_WIDTH)
IN_OFFS = tuple(int(v) for v in np.concatenate([[0], np.cumsum(IN_SIZES)]))

LANES = 128
FFT_N2 = 128

SEG_SIZES = (("mq", 384), ("mkv", 256), ("kr", 128), ("mg", 384), ("sq", 384), ("sk", 128),
             ("sv", 128), ("sg", 384), ("hu", 768), ("hg", 256))
SEG = {}
_o = 0
for _n, _s in SEG_SIZES:
    SEG[_n] = (_o, _o + _s)
    _o += _s
AUG_COLS = _o

SWA_ORDER = (0, 3, 1, 4, 2, 5)


def _silu(v):
    return v * jax.nn.sigmoid(v)


def _rms(v, g):
    return v * lax.rsqrt(jnp.mean(v * v, axis=-1, keepdims=True) + NORM_EPS) * g


def _swa_perm(a, axis, base):
    parts = [lax.slice_in_dim(a, base + SWA_DIM * h, base + SWA_DIM * (h + 1), axis=axis) for h in SWA_ORDER]
    return jnp.concatenate(parts, axis=axis)


def _prep_w_in(w_in):
    o = IN_OFFS
    mq, mkv, mkr, mg = (w_in[:, o[i]:o[i + 1]] for i in range(4))
    sq, sk, sv, sg, hu, hg = (w_in[:, o[i]:o[i + 1]] for i in range(4, 10))
    z = lambda n: jnp.zeros((D_MODEL, n), w_in.dtype)
    kr = jnp.concatenate([z(MLA_NOPE), mkr, z(32)], axis=1)
    w = jnp.concatenate([mq, mkv, kr, mg, _swa_perm(sq, 1, 0), sk, sv, _swa_perm(sg, 1, 0), hu, hg], axis=1)
    assert w.shape[1] == AUG_COLS
    return w.astype(BF16)


def _prep_w_out(w_out):
    return jnp.concatenate([w_out[:MLA_WIDTH], _swa_perm(w_out, 0, MLA_WIDTH), w_out[MLA_WIDTH + SWA_WIDTH:]],
                           axis=0).astype(BF16)


def _prep_w_uq(w_uq):
    zr = jnp.zeros((MLA_Q_RANK, 32), w_uq.dtype)
    cols = []
    for h in range(MLA_HEADS):
        cols += [w_uq[:, MLA_QK * h:MLA_QK * (h + 1)], zr]
    return jnp.concatenate(cols, axis=1).astype(BF16)


def _prep_w_ukv(w_ukv):
    zr = lambda n: jnp.zeros((MLA_KV_RANK, n), w_ukv.dtype)
    ks, vs = [], []
    for h in range(MLA_HEADS):
        ks += [w_ukv[:, 128 * h:128 * h + MLA_NOPE], zr(64)]
        vs += [w_ukv[:, 128 * h + MLA_NOPE:128 * (h + 1)]]
    return jnp.concatenate(ks + vs, axis=1).astype(BF16)


def _rope_tables(num_tokens, rot_dim):
    rows = num_tokens // GRID_W
    row = jnp.repeat(jnp.arange(rows, dtype=F32), GRID_W)
    col = jnp.tile(jnp.arange(GRID_W, dtype=F32), rows)
    n_freq = rot_dim // 4
    freqs = ROPE_THETA ** (-jnp.arange(n_freq, dtype=F32) / n_freq)
    ang = jnp.concatenate([row[:, None] * freqs, col[:, None] * freqs], axis=-1)
    return jnp.cos(ang), jnp.sin(ang)


def _mla_rope_lanes(cos, sin):
    n = cos.shape[0]
    z = lambda w: jnp.zeros((n, w), F32)
    c = jnp.concatenate([jnp.ones((n, MLA_NOPE), F32), cos, cos, z(32)], axis=1)
    sa = jnp.concatenate([z(MLA_NOPE), -sin, z(16), z(32)], axis=1)
    sb = jnp.concatenate([z(MLA_NOPE), z(16), sin, z(32)], axis=1)
    return c, sa, sb


def _swa_rope_lanes(cos, sin):
    z = jnp.zeros_like(sin)
    return (jnp.concatenate([cos] * 4, axis=1), jnp.concatenate([-sin, z, -sin, z], axis=1),
            jnp.concatenate([z, sin, z, sin], axis=1))


def _identity_rope_lanes(n):
    lane = jnp.arange(LANES)[None, :]
    zero = jnp.zeros((n, LANES), F32)
    return ((jnp.where(lane < MLA_QK, 1.0, 0.0) + zero, zero, zero), (zero + 1.0, zero, zero))


def _mod_kernel(c_ref, w_ref, b_ref, o_ref):
    s = _silu(c_ref[...])
    o_ref[...] = jnp.dot(s, w_ref[...], precision=HI, preferred_element_type=F32) + b_ref[...]


def _mod_call(cc, mod_w, mod_b):
    rows, n = cc.shape[0], mod_w.shape[1]
    tn = 512
    return pl.pallas_call(
        _mod_kernel,
        grid=(n // tn,),
        in_specs=[pl.BlockSpec((rows, D_MODEL), lambda j: (0, 0)),
                  pl.BlockSpec((D_MODEL, tn), lambda j: (0, j)),
                  pl.BlockSpec((1, tn), lambda j: (0, j))],
        out_specs=pl.BlockSpec((rows, tn), lambda j: (0, j)),
        out_shape=jax.ShapeDtypeStruct((rows, n), F32),
        name="mod",
    )(cc, mod_w, mod_b.reshape(1, n))


def _rope(x, tabs, sh):
    c, sa, sb = tabs
    return x * c + pltpu.roll(x, LANES - sh, 1) * sa + pltpu.roll(x, sh, 1) * sb


def _proj_kernel(x_ref, shift_ref, scale_ref, g_ref, w_ref, gq_ref, wuq_ref, gkv_ref, wukv_ref,
                 cm_ref, sam_ref, sbm_ref, cs_ref, sas_ref, sbs_ref,
                 qm_ref, km_ref, vm_ref, qs_ref, ks_ref, vs_ref, gate_ref, hu_ref):
    x = x_ref[0]
    y = _rms(x, g_ref[...])
    h = (y * (1.0 + scale_ref[0]) + shift_ref[0]).astype(BF16)

    def seg(first, last=None):
        a, b = SEG[first][0], SEG[last or first][1]
        return jnp.dot(h, w_ref[:, a:b], preferred_element_type=F32)

    tm_ = (cm_ref[...], sam_ref[...], sbm_ref[...])
    ts_ = (cs_ref[...], sas_ref[...], sbs_ref[...])
    log2e = math.log2(math.e)

    qn = _rms(seg("mq"), gq_ref[...]).astype(BF16)
    q2 = jnp.dot(qn, wuq_ref[...], preferred_element_type=F32)
    for hd in range(MLA_HEADS):
        blk = slice(LANES * hd, LANES * (hd + 1))
        qm_ref[0, :, blk] = (_rope(q2[:, blk], tm_, MLA_ROPE // 2) * (MLA_QK ** -0.5 * log2e)).astype(BF16)

    pkv = seg("mkv", "kr")
    kvn = _rms(pkv[:, :MLA_KV_RANK], gkv_ref[...]).astype(BF16)
    kv = jnp.dot(kvn, wukv_ref[...], preferred_element_type=F32)
    kr = _rope(pkv[:, MLA_KV_RANK:], tm_, MLA_ROPE // 2)
    half = MLA_HEADS * LANES
    for hd in range(MLA_HEADS):
        blk = slice(LANES * hd, LANES * (hd + 1))
        km_ref[0, :, blk] = (kv[:, blk] + kr).astype(BF16)
    vm_ref[0] = kv[:, half:].T.astype(BF16)

    sq = seg("sq")
    for j in range(3):
        blk = slice(LANES * j, LANES * (j + 1))
        qs_ref[0, :, blk] = (_rope(sq[:, blk], ts_, SWA_DIM // 2) * (SWA_DIM ** -0.5 * log2e)).astype(BF16)
    skv = seg("sk", "sv")
    ks_ref[0] = _rope(skv[:, :LANES], ts_, SWA_DIM // 2).astype(BF16)
    vs_ref[0] = skv[:, LANES:].T.astype(BF16)

    gate_ref[0, :, 0:384] = _silu(seg("mg")).astype(BF16)
    gate_ref[0, :, 384:768] = _silu(seg("sg")).astype(BF16)
    gate_ref[0, :, 768:1024] = _silu(seg("hg")).astype(BF16)
    hu_ref[0] = seg("hu")


def _proj_call(x, shift, scale, norm_g, w_aug, gq, wuq, gkv, wukv, rope_m, rope_s, tm):
    nb, n, _ = x.shape
    row = lambda b, i: (b, i, 0)
    col = lambda b, i: (b, 0, i)
    per_b = lambda b, i: (b, 0, 0)
    const = lambda b, i: (0, 0)
    tab = lambda b, i: (i, 0)
    kvw = SWA_KV_HEADS * SWA_DIM
    outs = [((n, 768), (tm, 768), row, BF16), ((n, 768), (tm, 768), row, BF16),
            ((MLA_WIDTH, n), (MLA_WIDTH, tm), col, BF16),
            ((n, SWA_WIDTH), (tm, SWA_WIDTH), row, BF16), ((n, kvw), (tm, kvw), row, BF16),
            ((kvw, n), (kvw, tm), col, BF16),
            ((n, D_MODEL), (tm, D_MODEL), row, BF16), ((n, 3 * HY_WIDTH), (tm, 3 * HY_WIDTH), row, F32)]
    return pl.pallas_call(
        _proj_kernel,
        grid=(nb, n // tm),
        in_specs=[pl.BlockSpec((1, tm, D_MODEL), row),
                  pl.BlockSpec((1, 1, D_MODEL), per_b), pl.BlockSpec((1, 1, D_MODEL), per_b),
                  pl.BlockSpec((1, D_MODEL), const),
                  pl.BlockSpec((D_MODEL, AUG_COLS), const),
                  pl.BlockSpec((1, MLA_Q_RANK), const), pl.BlockSpec(wuq.shape, const),
                  pl.BlockSpec((1, MLA_KV_RANK), const), pl.BlockSpec(wukv.shape, const)]
                 + [pl.BlockSpec((tm, LANES), tab)] * 6,
        out_specs=[pl.BlockSpec((1,) + blk, imap) for _, blk, imap, _ in outs],
        out_shape=[jax.ShapeDtypeStruct((nb,) + shp, dt) for shp, _, _, dt in outs],
        compiler_params=pltpu.CompilerParams(dimension_semantics=("parallel", "parallel")),
        name="proj",
    )(x, shift, scale, norm_g, w_aug, gq, wuq, gkv, wukv, *rope_m, *rope_s)


def _sublane_all(v, op):
    for sh in (4, 2, 1):
        v = op(v, pltpu.roll(v, sh, 0))
    return v


def _mla_kernel(*refs, n_lat, tk, has_lat, unroll):
    if has_lat:
        q_ref, kl_ref, vtl_ref, kc_ref, vtc_ref, o_ref, s_sc, mx_sc, m_sc, l_sc, acc_sc = refs
    else:
        q_ref, kc_ref, vtc_ref, o_ref, s_sc, mx_sc, m_sc, l_sc, acc_sc = refs
    tq = q_ref.shape[1]
    nt = (((1,), (1,)), ((), ()))
    heads = range(2)
    m_sc[...] = jnp.full(m_sc.shape, -1e30, F32)
    l_sc[...] = jnp.zeros(l_sc.shape, F32)
    acc_sc[...] = jnp.zeros(acc_sc.shape, F32)

    def scores(slot, hh, k):
        n = k.shape[0]
        q = q_ref[0, :, LANES * hh:LANES * (hh + 1)]
        s = lax.dot_general(k, q, nt, preferred_element_type=F32)
        s_sc[slot, hh, 0:n, :] = s
        mx_sc[slot, hh] = jnp.max(s.reshape(n // 8, 8, tq), axis=0)

    def update(slot, hh, vt):
        n = vt.shape[1]
        m_prev = m_sc[hh]
        m_new = jnp.maximum(m_prev, _sublane_all(mx_sc[slot, hh], jnp.maximum))
        alpha = jnp.exp2(m_prev - m_new)
        p3 = jnp.exp2(s_sc[slot, hh, 0:n, :].reshape(n // 8, 8, tq) - m_new[None])
        l_sc[hh] = alpha * l_sc[hh] + jnp.sum(p3, axis=0)
        pv = jnp.dot(vt, p3.reshape(n, tq).astype(BF16), preferred_element_type=F32)
        acc_sc[hh] = acc_sc[hh] * jnp.tile(alpha, (LANES // 8, 1)) + pv
        m_sc[hh] = m_new

    def k_lat(i, hh):
        return kl_ref[0, pl.ds(pl.multiple_of(i * tk, tk), tk), LANES * hh:LANES * (hh + 1)]

    def vt_lat(i):
        return vtl_ref[0, :, pl.ds(pl.multiple_of(i * tk, tk), tk)]

    n = n_lat // tk if has_lat else 0
    k_ctx = lambda hh: kc_ref[0, :, LANES * hh:LANES * (hh + 1)]
    if n == 0:
        for hh in heads:
            scores(0, hh, k_ctx(hh))
    else:
        for hh in heads:
            scores(0, hh, k_lat(0, hh))
        first = ((n - 1) // unroll) * unroll if unroll % 2 == 0 else 0
        if first:
            def body(j, carry):
                for t in range(unroll):
                    i = j * unroll + t
                    for hh in heads:
                        scores((t + 1) % 2, hh, k_lat(i + 1, hh))
                    for hh in heads:
                        update(t % 2, hh, vt_lat(i))
                return carry
            lax.fori_loop(0, first // unroll, body, 0)
        for i in range(first, n):
            for hh in heads:
                if i + 1 < n:
                    scores((i + 1) % 2, hh, k_lat(i + 1, hh))
                else:
                    scores((i + 1) % 2, hh, k_ctx(hh))
            for hh in heads:
                update(i % 2, hh, vt_lat(i))
    for hh in heads:
        update(n % 2, hh, vtc_ref[0])
    outs = []
    for hh in range(2):
        inv = 1.0 / _sublane_all(l_sc[hh], jnp.add)
        outs.append(acc_sc[hh] * jnp.tile(inv, (LANES // 8, 1)))
    row = lax.broadcasted_iota(jnp.int32, (LANES, tq), 0)
    o_ref[0] = jnp.where(row < MLA_V, outs[0], outs[1]).T.astype(o_ref.dtype)


def _mla_call(q, k_lat, vt_lat, k_ctx, vt_ctx, tq, tk, unroll=2):
    nb, nq, _ = q.shape
    nc = k_ctx.shape[1]
    has_lat = k_lat is not None
    qmap = lambda b, p, i: (b, i, p)
    kvmap = lambda b, p, i: (b, 0, p)
    vtmap = lambda b, p, i: (b, p, 0)
    in_specs = [pl.BlockSpec((1, tq, 2 * LANES), qmap)]
    args = [q]
    n_lat = 0
    if has_lat:
        n_lat = k_lat.shape[1]
        in_specs += [pl.BlockSpec((1, n_lat, 2 * LANES), kvmap), pl.BlockSpec((1, LANES, n_lat), vtmap)]
        args += [k_lat, vt_lat]
    in_specs += [pl.BlockSpec((1, nc, 2 * LANES), kvmap), pl.BlockSpec((1, LANES, nc), vtmap)]
    args += [k_ctx, vt_ctx]
    return pl.pallas_call(
        functools.partial(_mla_kernel, n_lat=n_lat, tk=tk, has_lat=has_lat, unroll=unroll),
        grid=(nb, MLA_HEADS // 2, nq // tq),
        in_specs=in_specs,
        out_specs=pl.BlockSpec((1, tq, LANES), qmap),
        out_shape=jax.ShapeDtypeStruct((nb, nq, MLA_WIDTH), BF16),
        scratch_shapes=[pltpu.VMEM((2, 2, max(tk, nc), tq), F32), pltpu.VMEM((2, 2, 8, tq), F32),
                        pltpu.VMEM((2, 8, tq), F32), pltpu.VMEM((2, 8, tq), F32), pltpu.VMEM((2, LANES, tq), F32)],
        compiler_params=pltpu.CompilerParams(dimension_semantics=("parallel", "parallel", "parallel")),
        name="mla_lat" if has_lat else "mla_ctx",
    )(*args)


def _swa_kernel(*refs, seq, band):
    if band:
        sink_ref, q_ref, kp_ref, kc_ref, kn_ref, vp_ref, vc_ref, vn_ref, kx_ref, vx_ref, o_ref, s_sc, mx_sc = refs
    else:
        sink_ref, q_ref, kx_ref, vx_ref, o_ref, s_sc, mx_sc = refs
    tq = q_ref.shape[1]
    nt = (((1,), (1,)), ((), ()))
    lane = lax.broadcasted_iota(jnp.int32, (tq, LANES), 1)
    row = lax.broadcasted_iota(jnp.int32, (LANES, tq), 0)
    keys, vals = kx_ref[0], vx_ref[0]
    nk = keys.shape[0]
    if band:
        i = pl.program_id(1)
        keys = jnp.concatenate([keys, kp_ref[0], kc_ref[0], kn_ref[0]], axis=0)
        vals = jnp.concatenate([vals, vp_ref[0], vc_ref[0], vn_ref[0]], axis=1)
        nb = tq + 2 * SWA_WINDOW
        kpos = i * tq - SWA_WINDOW + lax.broadcasted_iota(jnp.int32, (nb, tq), 0)
        qpos = i * tq + lax.broadcasted_iota(jnp.int32, (nb, tq), 1)
        valid = (jnp.abs(kpos - qpos) <= SWA_WINDOW) & (kpos >= 0) & (kpos < seq)

    for j in range(SWA_HEADS // 2):
        qp = q_ref[0, :, LANES * j:LANES * (j + 1)]
        for g in range(SWA_KV_HEADS):
            q = jnp.where((lane >= SWA_DIM * g) & (lane < SWA_DIM * (g + 1)), qp, jnp.zeros_like(qp))
            s = lax.dot_general(keys, q, nt, preferred_element_type=F32)
            if band:
                s = jnp.concatenate([s[:nk], jnp.where(valid, s[nk:], -1e30)], axis=0)
            s_sc[2 * j + g] = s
            mx_sc[2 * j + g] = jnp.max(s.reshape(-1, 8, tq), axis=0)
    for j in range(SWA_HEADS // 2):
        res = []
        for g in range(SWA_KV_HEADS):
            sink = sink_ref[SWA_ORDER[2 * j + g]] * math.log2(math.e)
            m = jnp.maximum(_sublane_all(mx_sc[2 * j + g], jnp.maximum), sink)
            p3 = jnp.exp2(s_sc[2 * j + g].reshape(-1, 8, tq) - m[None])
            denom = _sublane_all(jnp.sum(p3, axis=0), jnp.add) + jnp.exp2(sink - m)
            o = jnp.dot(vals, p3.reshape(-1, tq).astype(BF16), preferred_element_type=F32)
            res.append(o * jnp.tile(1.0 / denom, (LANES // 8, 1)))
        o_ref[0, :, LANES * j:LANES * (j + 1)] = jnp.where(row < SWA_DIM, res[0], res[1]).T.astype(o_ref.dtype)


def _swa_call(sink, q, k, vt, k_ctx, vt_ctx, band):
    nb, nq, _ = q.shape
    nc = k_ctx.shape[1]
    tq = min(nq, 256) if band else nq
    nblk = nq // tq
    nkeys = nc + (tq + 2 * SWA_WINDOW if band else 0)
    wb = tq // SWA_WINDOW
    nwb = nq // SWA_WINDOW
    kw = SWA_KV_HEADS * SWA_DIM
    cur = lambda b, i: (b, i, 0)
    ctx = lambda b, i: (b, 0, 0)
    in_specs = [pl.BlockSpec(memory_space=pltpu.SMEM), pl.BlockSpec((1, tq, SWA_WIDTH), cur)]
    args = [sink, q]
    if band:
        in_specs += [pl.BlockSpec((1, SWA_WINDOW, kw), lambda b, i: (b, jnp.maximum(i * wb - 1, 0), 0)),
                     pl.BlockSpec((1, tq, kw), cur),
                     pl.BlockSpec((1, SWA_WINDOW, kw), lambda b, i: (b, jnp.minimum((i + 1) * wb, nwb - 1), 0)),
                     pl.BlockSpec((1, kw, SWA_WINDOW), lambda b, i: (b, 0, jnp.maximum(i * wb - 1, 0))),
                     pl.BlockSpec((1, kw, tq), lambda b, i: (b, 0, i)),
                     pl.BlockSpec((1, kw, SWA_WINDOW), lambda b, i: (b, 0, jnp.minimum((i + 1) * wb, nwb - 1)))]
        args += [k, k, k, vt, vt, vt]
    in_specs += [pl.BlockSpec((1, nc, kw), ctx), pl.BlockSpec((1, kw, nc), ctx)]
    args += [k_ctx, vt_ctx]
    return pl.pallas_call(
        functools.partial(_swa_kernel, seq=nq, band=band),
        grid=(nb, nblk),
        in_specs=in_specs,
        out_specs=pl.BlockSpec((1, tq, SWA_WIDTH), cur),
        out_shape=jax.ShapeDtypeStruct((nb, nq, SWA_WIDTH), BF16),
        scratch_shapes=[pltpu.VMEM((SWA_HEADS, nkeys, tq), F32), pltpu.VMEM((SWA_HEADS, 8, tq), F32)],
        compiler_params=pltpu.CompilerParams(dimension_semantics=("parallel", "parallel")),
        name="swa_lat" if band else "swa_ctx",
    )(*args)


def _filter_kernel(z_ref, w1_ref, b1_ref, fr_ref, w2_ref, b2_ref, w3_ref, b3_ref, dl_ref,
                   hf_ref, hb_ref, nrm_ref, *, num_tokens):
    i = pl.program_id(0)
    tl = z_ref.shape[0]
    fr = fr_ref[...]
    h = jnp.sin(fr * (jnp.dot(z_ref[...], w1_ref[...], precision=HI, preferred_element_type=F32) + b1_ref[...]))
    h = jnp.sin(fr * (jnp.dot(h, w2_ref[...], precision=HI, preferred_element_type=F32) + b2_ref[...]))
    h = jnp.dot(h, w3_ref[...], precision=HI, preferred_element_type=F32) + b3_ref[...]
    row = i * tl + lax.broadcasted_iota(jnp.int32, (tl, HY_WIDTH), 0)
    t = row.astype(F32) * (1.0 / (num_tokens - 1))
    decay = jnp.exp(-t * dl_ref[...])
    hf = h[:, :HY_WIDTH] * decay
    hb = jnp.where(row == 0, 0.0, h[:, HY_WIDTH:] * decay)
    hf_ref[...] = hf
    hb_ref[...] = hb
    part = jnp.sum(jnp.abs(hf) + jnp.abs(hb), axis=0, keepdims=True)

    @pl.when(i == 0)
    def _():
        nrm_ref[...] = jnp.zeros(nrm_ref.shape, F32)

    nrm_ref[...] += part


def _filter_call(lp, num_tokens):
    tl = min(num_tokens, 1024)
    t = jnp.linspace(0.0, 1.0, num_tokens, dtype=F32)[:, None]
    w = (2.0 * math.pi / num_tokens) * jnp.arange(num_tokens, dtype=F32)[:, None]
    bands = jnp.linspace(1e-4, HY_BANDS - 1, HY_BANDS, dtype=F32)[None, :]
    z = jnp.concatenate([t, jnp.cos(bands * w), -jnp.sin(bands * w),
                         jnp.zeros((num_tokens, LANES - HY_EMB), F32)], axis=-1)
    w1 = jnp.concatenate([lp["hy_w1"], jnp.zeros((LANES - HY_EMB, HY_HIDDEN), F32)], axis=0)
    deltas = jnp.abs(jnp.linspace(HY_DECAY_FAST, HY_DECAY_SLOW, HY_WIDTH, dtype=F32)).reshape(1, HY_WIDTH)
    const = lambda i: (0, 0)
    rowm = lambda i: (i, 0)
    r1 = lambda a: a.reshape(1, -1)
    return pl.pallas_call(
        functools.partial(_filter_kernel, num_tokens=num_tokens),
        grid=(num_tokens // tl,),
        in_specs=[pl.BlockSpec((tl, LANES), rowm),
                  pl.BlockSpec((LANES, HY_HIDDEN), const), pl.BlockSpec((1, HY_HIDDEN), const),
                  pl.BlockSpec((1, HY_HIDDEN), const),
                  pl.BlockSpec((HY_HIDDEN, HY_HIDDEN), const), pl.BlockSpec((1, HY_HIDDEN), const),
                  pl.BlockSpec((HY_HIDDEN, 2 * HY_WIDTH), const), pl.BlockSpec((1, 2 * HY_WIDTH), const),
                  pl.BlockSpec((1, HY_WIDTH), const)],
        out_specs=[pl.BlockSpec((tl, HY_WIDTH), rowm), pl.BlockSpec((tl, HY_WIDTH), rowm),
                   pl.BlockSpec((1, HY_WIDTH), const)],
        out_shape=[jax.ShapeDtypeStruct((num_tokens, HY_WIDTH), F32)] * 2 + [jax.ShapeDtypeStruct((1, HY_WIDTH), F32)],
        compiler_params=pltpu.CompilerParams(dimension_semantics=("arbitrary",)),
        name="hy_filter",
    )(z, w1, r1(lp["hy_b1"]), r1(lp["hy_freq"]), lp["hy_w2"], r1(lp["hy_b2"]), lp["hy_w3"], r1(lp["hy_b3"]), deltas)


def _short_conv(u, prev_row, next_row, w_ref, b_ref):
    n = u.shape[0]
    row = lax.broadcasted_iota(jnp.int32, u.shape, 0)
    up = jnp.where(row == 0, prev_row, pltpu.roll(u, 1, 0))
    un = jnp.where(row == n - 1, next_row, pltpu.roll(u, n - 1, 0))
    return up * w_ref[0:1, :] + u * w_ref[1:2, :] + un * w_ref[2:3, :] + b_ref[...]


def _hconv_kernel(u_ref, up_ref, un_ref, w_ref, b_ref, x0_ref, z_ref, zb_ref):
    i = pl.program_id(1)
    last = pl.num_programs(1) - 1
    prev_row = jnp.where(i == 0, 0.0, up_ref[0, 7:8, :])
    next_row = jnp.where(i == last, 0.0, un_ref[0, 0:1, :])
    uc = _short_conv(u_ref[0], prev_row, next_row, w_ref, b_ref)
    x0_ref[0] = uc[:, :HY_WIDTH]
    z = uc[:, 2 * HY_WIDTH:] * uc[:, HY_WIDTH:2 * HY_WIDTH]
    z_ref[0] = z
    zb_ref[0] = z.astype(BF16)


def _hconv_call(hu, conv_w, conv_b, tl):
    nb, n, w3 = hu.shape
    hb = tl // 8
    nh = n // 8
    cur = lambda b, i: (b, i, 0)
    prev = lambda b, i: (b, jnp.maximum(i * hb - 1, 0), 0)
    nxt = lambda b, i: (b, jnp.minimum((i + 1) * hb, nh - 1), 0)
    const = lambda b, i: (0, 0)
    return pl.pallas_call(
        _hconv_kernel,
        grid=(nb, n // tl),
        in_specs=[pl.BlockSpec((1, tl, w3), cur), pl.BlockSpec((1, 8, w3), prev), pl.BlockSpec((1, 8, w3), nxt),
                  pl.BlockSpec((3, w3), const), pl.BlockSpec((1, w3), const)],
        out_specs=[pl.BlockSpec((1, tl, HY_WIDTH), cur)] * 3,
        out_shape=[jax.ShapeDtypeStruct((nb, n, HY_WIDTH), d) for d in (F32, F32, BF16)],
        compiler_params=pltpu.CompilerParams(dimension_semantics=("parallel", "parallel")),
        name="hy_conv",
    )(hu, hu, hu, conv_w, conv_b.reshape(1, w3))


def _dft_tables(seq):
    n1 = 2 * seq // FFT_N2
    nd = seq // FFT_N2
    n = n1 * FFT_N2
    k1 = np.arange(n1)[:, None]
    a = 2.0 * np.pi * ((k1 * np.arange(nd)[None, :]) % n1) / n1
    c, s = np.cos(a), np.sin(a)
    fwd1 = np.block([[c, s], [-s, c]])
    inv1 = np.block([[c.T, -s.T], [s.T, c.T]])
    k2 = np.arange(FFT_N2)[None, :, None]
    n2 = np.arange(FFT_N2)[None, None, :]
    g = 2.0 * np.pi * ((n2 * (k1[:, :, None] + n1 * k2)) % n) / n
    gr, gi = np.cos(g), -np.sin(g)
    cast = lambda t: jnp.asarray(np.ascontiguousarray(t, dtype=np.float32), BF16)
    return (cast(fwd1), cast(inv1), cast(gr), cast(gi),
            cast(np.transpose(gr, (0, 2, 1))), cast(np.transpose(-gi, (0, 2, 1))))


def _stage1_kernel(m_ref, z_ref, a_ref):
    zs = jnp.concatenate([z_ref[0, 0], z_ref[0, 1]], axis=0).astype(BF16)
    a_ref[0] = jnp.dot(m_ref[...], zs, preferred_element_type=F32).astype(a_ref.dtype)


def _stage1_call(mat, z, out_dtype, tn, name):
    npair, _, rows, cols = z.shape
    mo = mat.shape[0]
    return pl.pallas_call(
        _stage1_kernel,
        grid=(npair, cols // tn),
        in_specs=[pl.BlockSpec(mat.shape, lambda p, j: (0, 0)),
                  pl.BlockSpec((1, 2, rows, tn), lambda p, j: (p, 0, 0, j))],
        out_specs=pl.BlockSpec((1, mo, tn), lambda p, j: (p, 0, j)),
        out_shape=jax.ShapeDtypeStruct((npair, mo, cols), out_dtype),
        compiler_params=pltpu.CompilerParams(dimension_semantics=("parallel", "parallel")),
        name=name,
    )(mat, z)


def _cplx_mat(re, im):
    return jnp.concatenate([jnp.concatenate([re, -im], axis=1), jnp.concatenate([im, re], axis=1)], axis=0)


def _spec_kernel(gr_ref, gi_ref, a_ref, nrm_ref, k_ref, *, kb, scale):
    inv = scale / (nrm_ref[...] + NORM_EPS)
    for j in range(kb):
        g2 = _cplx_mat(gr_ref[j], gi_ref[j])
        x1 = jnp.dot(g2, jnp.concatenate([a_ref[0, 0, j], a_ref[0, 1, j]], axis=0), preferred_element_type=F32)
        x2 = jnp.dot(g2, jnp.concatenate([a_ref[1, 0, j], a_ref[1, 1, j]], axis=0), preferred_element_type=F32)
        k_ref[j, :FFT_N2, :] = (x1[:FFT_N2] + x2[:FFT_N2]) * inv
        k_ref[j, FFT_N2:, :] = (x1[FFT_N2:] - x2[FFT_N2:]) * inv


def _spec_call(gr, gi, a, nrm, kb, scale):
    n1 = gr.shape[0]
    tab = lambda i: (i, 0, 0)
    return pl.pallas_call(
        functools.partial(_spec_kernel, kb=kb, scale=scale),
        grid=(n1 // kb,),
        in_specs=[pl.BlockSpec((kb, FFT_N2, FFT_N2), tab), pl.BlockSpec((kb, FFT_N2, FFT_N2), tab),
                  pl.BlockSpec((2, 2, kb, FFT_N2, HY_WIDTH), lambda i: (0, 0, i, 0, 0)),
                  pl.BlockSpec((1, HY_WIDTH), lambda i: (0, 0))],
        out_specs=pl.BlockSpec((kb, 2 * FFT_N2, HY_WIDTH), tab),
        out_shape=jax.ShapeDtypeStruct((n1, 2 * FFT_N2, HY_WIDTH), F32),
        compiler_params=pltpu.CompilerParams(dimension_semantics=("parallel",)),
        name="hy_spec",
    )(gr, gi, a, nrm)


def _stage2_kernel(gr_ref, gi_ref, hr_ref, hi_ref, k_ref, a_ref, b_ref, *, kb):
    for j in range(kb):
        g2 = _cplx_mat(gr_ref[j], gi_ref[j])
        h2 = _cplx_mat(hr_ref[j], hi_ref[j])
        kr, ki = k_ref[j, :FFT_N2, :], k_ref[j, FFT_N2:, :]
        for p in range(a_ref.shape[0]):
            s = jnp.concatenate([a_ref[p, 0, j], a_ref[p, 1, j]], axis=0)
            x = jnp.dot(g2, s, preferred_element_type=F32)
            xr, xi = x[:FFT_N2], x[FFT_N2:]
            y = jnp.concatenate([xr * kr - xi * ki, xr * ki + xi * kr], axis=0).astype(BF16)
            b = jnp.dot(h2, y, preferred_element_type=F32)
            b_ref[p, 0, j] = b[:FFT_N2].astype(b_ref.dtype)
            b_ref[p, 1, j] = b[FFT_N2:].astype(b_ref.dtype)


def _stage2_call(gr, gi, hr, hi, kf, a, kb):
    npair = a.shape[0]
    n1 = gr.shape[0]
    tab = lambda i: (i, 0, 0)
    dat = lambda i: (0, 0, i, 0, 0)
    return pl.pallas_call(
        functools.partial(_stage2_kernel, kb=kb),
        grid=(n1 // kb,),
        in_specs=[pl.BlockSpec((kb, FFT_N2, FFT_N2), tab)] * 4
                 + [pl.BlockSpec((kb, 2 * FFT_N2, HY_WIDTH), tab),
                    pl.BlockSpec((npair, 2, kb, FFT_N2, HY_WIDTH), dat)],
        out_specs=pl.BlockSpec((npair, 2, kb, FFT_N2, HY_WIDTH), dat),
        out_shape=jax.ShapeDtypeStruct(a.shape, BF16),
        compiler_params=pltpu.CompilerParams(dimension_semantics=("parallel",)),
        name="hy_stage2",
    )(gr, gi, hr, hi, kf, a)


def _long_conv(z, hf, hb, nrm, tables):
    nb, seq, w = z.shape
    fwd1, inv1, gr, gi, hr, hi = tables
    n1 = 2 * seq // FFT_N2
    nd = seq // FFT_N2
    cols = FFT_N2 * w
    tn = min(cols, 4096)
    kb = min(n1, 8)
    zero = jnp.zeros_like(hf)
    zf = jnp.stack([jnp.stack([hf, zero]), jnp.stack([hb, zero])]).reshape(2, 2, nd, cols)
    af = _stage1_call(fwd1, zf, BF16, tn, "hy_fstage1").reshape(2, 2, n1, FFT_N2, w)
    kf = _spec_call(gr, gi, af, nrm, kb, 1.0 / (n1 * FFT_N2))
    zd = z.reshape(nb // 2, 2, nd, cols)
    a = _stage1_call(fwd1, zd, BF16, tn, "hy_stage1").reshape(nb // 2, 2, n1, FFT_N2, w)
    b = _stage2_call(gr, gi, hr, hi, kf, a, kb).reshape(nb // 2, 2, n1, cols)
    y = _stage1_call(inv1, b, BF16, tn, "hy_istage1")
    return y.reshape(nb, seq, w)


def _hctx_kernel(u_ref, w_ref, b_ref, hf_ref, hb_ref, nrm_ref, c_ref, s_ref, ct_ref, st_ref,
                 x0_ref, z_ref, y_ref):
    u = u_ref[0]
    n = u.shape[0]
    zero_row = jnp.zeros((1, u.shape[1]), F32)
    uc = _short_conv(u, zero_row, zero_row, w_ref, b_ref)
    x0 = uc[:, :HY_WIDTH]
    z = uc[:, 2 * HY_WIDTH:] * uc[:, HY_WIDTH:2 * HY_WIDTH]
    dot = lambda a, b: jnp.dot(a, b, precision=HI, preferred_element_type=F32)
    c, s = c_ref[...], s_ref[...]
    inv = (1.0 / (2 * n)) / (nrm_ref[...] + NORM_EPS)
    kr = (dot(c, hf_ref[...]) + dot(c, hb_ref[...])) * inv
    ki = (dot(s, hb_ref[...]) - dot(s, hf_ref[...])) * inv
    xr, xi = dot(c, z), -dot(s, z)
    yr, yi = xr * kr - xi * ki, xr * ki + xi * kr
    x0_ref[0] = x0
    z_ref[0] = z
    y_ref[0] = dot(ct_ref[...], yr) - dot(st_ref[...], yi)


def _hctx_call(hu, conv_w, conv_b, hf, hb, nrm):
    nb, n, w3 = hu.shape
    k = np.arange(2 * n)[:, None]
    a = 2.0 * np.pi * ((k * np.arange(n)[None, :]) % (2 * n)) / (2 * n)
    c, s = np.cos(a).astype(np.float32), np.sin(a).astype(np.float32)
    const = lambda b: (0, 0)
    cur = lambda b: (b, 0, 0)
    out = pl.BlockSpec((1, n, HY_WIDTH), cur)
    return pl.pallas_call(
        _hctx_kernel,
        grid=(nb,),
        in_specs=[pl.BlockSpec((1, n, w3), cur), pl.BlockSpec((3, w3), const), pl.BlockSpec((1, w3), const),
                  pl.BlockSpec((n, HY_WIDTH), const), pl.BlockSpec((n, HY_WIDTH), const),
                  pl.BlockSpec((1, HY_WIDTH), const),
                  pl.BlockSpec((2 * n, n), const), pl.BlockSpec((2 * n, n), const),
                  pl.BlockSpec((n, 2 * n), const), pl.BlockSpec((n, 2 * n), const)],
        out_specs=[out, out, out],
        out_shape=[jax.ShapeDtypeStruct((nb, n, HY_WIDTH), F32)] * 3,
        compiler_params=pltpu.CompilerParams(dimension_semantics=("parallel",)),
        name="hy_ctx",
    )(hu, conv_w, conv_b.reshape(1, w3), hf, hb, nrm, jnp.asarray(c), jnp.asarray(s),
      jnp.asarray(c.T.copy()), jnp.asarray(s.T.copy()))


def _merge_kernel(x_ref, oa_ref, ob_ref, x0_ref, z_ref, yc_ref, gate_ref, gx_ref, bias_ref, w_ref, fg_ref,
                  o_ref, *, final):
    g = gate_ref[0].astype(F32)
    ya = oa_ref[0].astype(F32) * g[:, 0:384]
    yb = ob_ref[0].astype(F32) * g[:, 384:768]
    z = z_ref[0]
    yh = x0_ref[0] * (yc_ref[0].astype(F32) + bias_ref[...] * z) * g[:, 768:1024]
    y = jnp.concatenate([ya, yb, yh], axis=1).astype(BF16)
    xn = x_ref[0] + gx_ref[0] * jnp.dot(y, w_ref[...], preferred_element_type=F32)
    if final:
        xn = _rms(xn, fg_ref[...])
    o_ref[0] = xn


def _merge_call(x, oa, ob, x0, z, yc, gates, gate_x, bias, w_out, fg, tm, final):
    nb, n, _ = x.shape
    row = lambda b, i: (b, i, 0)
    per_b = lambda b, i: (b, 0, 0)
    const = lambda b, i: (0, 0)
    return pl.pallas_call(
        functools.partial(_merge_kernel, final=final),
        grid=(nb, n // tm),
        in_specs=[pl.BlockSpec((1, tm, D_MODEL), row),
                  pl.BlockSpec((1, tm, MLA_WIDTH), row), pl.BlockSpec((1, tm, SWA_WIDTH), row),
                  pl.BlockSpec((1, tm, HY_WIDTH), row), pl.BlockSpec((1, tm, HY_WIDTH), row),
                  pl.BlockSpec((1, tm, HY_WIDTH), row),
                  pl.BlockSpec((1, tm, D_MODEL), row), pl.BlockSpec((1, 1, D_MODEL), per_b),
                  pl.BlockSpec((1, HY_WIDTH), const), pl.BlockSpec((D_MIX, D_MODEL), const),
                  pl.BlockSpec((1, D_MODEL), const)],
        out_specs=pl.BlockSpec((1, tm, D_MODEL), row),
        out_shape=jax.ShapeDtypeStruct((nb, n, D_MODEL), F32),
        compiler_params=pltpu.CompilerParams(dimension_semantics=("parallel", "parallel")),
        name="merge",
    )(x, oa, ob, x0, z, yc, gates, gate_x, bias, w_out, fg)


def _layer(x, xc, mod, lp, consts, update_ctx, final, final_g):
    nb, seq, _ = x.shape
    nctx = xc.shape[1]
    w_aug = _prep_w_in(lp["w_in"])
    w_out = _prep_w_out(lp["w_out"])
    wuq = _prep_w_uq(lp["mla_w_uq"])
    wukv = _prep_w_ukv(lp["mla_w_ukv"])
    r1 = lambda a: a.reshape(1, -1)
    shift, scale, gate = (mod[:, D_MODEL * j:D_MODEL * (j + 1)] for j in range(3))
    sel = lambda a, lo, hi: a[lo:hi].reshape(hi - lo, 1, D_MODEL)
    ctx_b = lambda a: jnp.broadcast_to(a[nb:nb + 1].reshape(1, 1, D_MODEL), (nb, 1, D_MODEL))

    tm = min(seq, 512)
    px = _proj_call(x, sel(shift, 0, nb), sel(scale, 0, nb), r1(lp["norm_g"]), w_aug, r1(lp["mla_q_norm"]), wuq,
                    r1(lp["mla_kv_norm"]), wukv, *consts["rope_lat"], tm)
    pc = _proj_call(xc, ctx_b(shift), ctx_b(scale), r1(lp["norm_g"]), w_aug, r1(lp["mla_q_norm"]), wuq,
                    r1(lp["mla_kv_norm"]), wukv, *consts["rope_ctx"], nctx)
    qm, km, vm, qs, ks, vs, gates, hu = px
    qm_c, km_c, vm_c, qs_c, ks_c, vs_c, gates_c, hu_c = pc

    o_a = _mla_call(qm, km, vm, km_c, vm_c, min(seq, 512), 1024 if seq >= 2048 else 128, 2)
    o_b = _swa_call(lp["swa_sink"], qs, ks, vs, ks_c, vs_c, True)

    hf, hb, nrm = _filter_call(lp, seq)
    x0, z, zb = _hconv_call(hu, lp["hy_conv_w"], lp["hy_conv_b"], min(seq, 1024))
    yc = _long_conv(zb, hf, hb, nrm, consts["dft"])

    x_new = _merge_call(x, o_a, o_b, x0, z, yc, gates, sel(gate, 0, nb), r1(lp["hy_bias"]),
                        w_out, r1(final_g), tm, final)
    if update_ctx:
        oc_a = _mla_call(qm_c, None, None, km_c, vm_c, nctx, nctx)
        oc_b = _swa_call(lp["swa_sink"], qs_c, None, None, ks_c, vs_c, False)
        hf_c, hb_c, nrm_c = _filter_call(lp, nctx)
        x0_c, z_c, yc_c = _hctx_call(hu_c, lp["hy_conv_w"], lp["hy_conv_b"], hf_c, hb_c, nrm_c)
        xc = _merge_call(xc, oc_a, oc_b, x0_c, z_c, yc_c, gates_c, ctx_b(gate), r1(lp["hy_bias"]),
                         w_out, r1(final_g), nctx, False)
    return x_new, xc


def _forward(x, c, ctx, c_ctx, params, final_norm_g):
    nb, seq, _ = x.shape
    nctx = ctx.shape[1]
    consts = {"rope_lat": (_mla_rope_lanes(*_rope_tables(seq, MLA_ROPE)), _swa_rope_lanes(*_rope_tables(seq, SWA_DIM))),
              "rope_ctx": _identity_rope_lanes(nctx), "dft": _dft_tables(seq)}
    cc = jnp.concatenate([c, c_ctx[None, :], jnp.zeros((8 - nb - 1, D_MODEL), F32)], axis=0)
    xc = ctx
    depth = params["w_in"].shape[0]
    for l in range(depth):
        lp = {k: v[l] for k, v in params.items()}
        mod = _mod_call(cc, lp["mod_w"], lp["mod_b"])
        x, xc = _layer(x, xc, mod, lp, consts, l < depth - 1, l == depth - 1, final_norm_g)
    return x


def kernel(x, c, ctx, c_ctx, norm_g, mod_w, mod_b, w_in, mla_q_norm, mla_w_uq, mla_kv_norm, mla_w_ukv, swa_sink,
           hy_conv_w, hy_conv_b, hy_w1, hy_b1, hy_freq, hy_w2, hy_b2, hy_w3, hy_b3, hy_bias, w_out, final_norm_g):
    params = dict(norm_g=norm_g, mod_w=mod_w, mod_b=mod_b, w_in=w_in, mla_q_norm=mla_q_norm, mla_w_uq=mla_w_uq,
                  mla_kv_norm=mla_kv_norm, mla_w_ukv=mla_w_ukv, swa_sink=swa_sink, hy_conv_w=hy_conv_w,
                  hy_conv_b=hy_conv_b, hy_w1=hy_w1, hy_b1=hy_b1, hy_freq=hy_freq, hy_w2=hy_w2, hy_b2=hy_b2,
                  hy_w3=hy_w3, hy_b3=hy_b3, hy_bias=hy_bias, w_out=w_out)
    return _forward(x, c, ctx, c_ctx, params, final_norm_g)
```

```python
import functools
import math

import numpy as np
import jax
import jax.numpy as jnp
from jax import lax
from jax.experimental import pallas as pl
from jax.experimental.pallas import tpu as pltpu

F32 = jnp.float32
BF16 = jnp.bfloat16
HI = lax.Precision.HIGHEST

D_MODEL = 1024
DEPTH = 2
GRID_W = 64
NORM_EPS = 1e-6
ROPE_THETA = 10000.0

MLA_HEADS = 6
MLA_NOPE = 64
MLA_ROPE = 32
MLA_QK = MLA_NOPE + MLA_ROPE
MLA_V = 64
MLA_Q_RANK = 384
MLA_KV_RANK = 256
MLA_WIDTH = MLA_HEADS * MLA_V

SWA_HEADS = 6
SWA_KV_HEADS = 2
SWA_GROUP = SWA_HEADS // SWA_KV_HEADS
SWA_DIM = 64
SWA_WINDOW = 128
SWA_WIDTH = SWA_HEADS * SWA_DIM

HY_WIDTH = 256
HY_BANDS = 8
HY_EMB = 1 + 2 * HY_BANDS
HY_HIDDEN = 64
HY_DECAY_FAST = math.log(1e-2) / 0.3
HY_DECAY_SLOW = math.log(1e-2) / 1.5

D_MIX = MLA_WIDTH + SWA_WIDTH + HY_WIDTH
IN_SIZES = (MLA_Q_RANK, MLA_KV_RANK, MLA_ROPE, MLA_WIDTH, SWA_WIDTH, SWA_KV_HEADS * SWA_DIM,
            SWA_KV_HEADS * SWA_DIM, SWA_WIDTH, 3 * HY_WIDTH, HY_WIDTH)
IN_OFFS = tuple(int(v) for v in np.concatenate([[0], np.cumsum(IN_SIZES)]))

LANES = 128
FFT_N2 = 128

SEG_SIZES = (("mq", 384), ("mkv", 256), ("kr", 128), ("mg", 384), ("sq", 384), ("sk", 128),
             ("sv", 128), ("sg", 384), ("hu", 768), ("hg", 256))
SEG = {}
_o = 0
for _n, _s in SEG_SIZES:
    SEG[_n] = (_o, _o + _s)
    _o += _s
AUG_COLS = _o

SWA_ORDER = (0, 3, 1, 4, 2, 5)


def _silu(v):
    return v * jax.nn.sigmoid(v)


def _rms(v, g):
    return v * lax.rsqrt(jnp.mean(v * v, axis=-1, keepdims=True) + NORM_EPS) * g


def _swa_perm(a, axis, base):
    parts = [lax.slice_in_dim(a, base + SWA_DIM * h, base + SWA_DIM * (h + 1), axis=axis) for h in SWA_ORDER]
    return jnp.concatenate(parts, axis=axis)


def _prep_w_in(w_in):
    o = IN_OFFS
    mq, mkv, mkr, mg = (w_in[:, o[i]:o[i + 1]] for i in range(4))
    sq, sk, sv, sg, hu, hg = (w_in[:, o[i]:o[i + 1]] for i in range(4, 10))
    z = lambda n: jnp.zeros((D_MODEL, n), w_in.dtype)
    kr = jnp.concatenate([z(MLA_NOPE), mkr, z(32)], axis=1)
    w = jnp.concatenate([mq, mkv, kr, mg, _swa_perm(sq, 1, 0), sk, sv, _swa_perm(sg, 1, 0), hu, hg], axis=1)
    assert w.shape[1] == AUG_COLS
    return w.astype(BF16)


def _prep_w_out(w_out):
    return jnp.concatenate([w_out[:MLA_WIDTH], _swa_perm(w_out, 0, MLA_WIDTH), w_out[MLA_WIDTH + SWA_WIDTH:]],
                           axis=0).astype(BF16)


def _prep_w_uq(w_uq):
    zr = jnp.zeros((MLA_Q_RANK, 32), w_uq.dtype)
    cols = []
    for h in range(MLA_HEADS):
        cols += [w_uq[:, MLA_QK * h:MLA_QK * (h + 1)], zr]
    return jnp.concatenate(cols, axis=1).astype(BF16)


def _prep_w_ukv(w_ukv):
    zr = lambda n: jnp.zeros((MLA_KV_RANK, n), w_ukv.dtype)
    ks, vs = [], []
    for h in range(MLA_HEADS):
        ks += [w_ukv[:, 128 * h:128 * h + MLA_NOPE], zr(64)]
        vs += [w_ukv[:, 128 * h + MLA_NOPE:128 * (h + 1)]]
    return jnp.concatenate(ks + vs, axis=1).astype(BF16)


def _rope_tables(num_tokens, rot_dim):
    rows = num_tokens // GRID_W
    row = jnp.repeat(jnp.arange(rows, dtype=F32), GRID_W)
    col = jnp.tile(jnp.arange(GRID_W, dtype=F32), rows)
    n_freq = rot_dim // 4
    freqs = ROPE_THETA ** (-jnp.arange(n_freq, dtype=F32) / n_freq)
    ang = jnp.concatenate([row[:, None] * freqs, col[:, None] * freqs], axis=-1)
    return jnp.cos(ang), jnp.sin(ang)


def _mla_rope_lanes(cos, sin):
    n = cos.shape[0]
    z = lambda w: jnp.zeros((n, w), F32)
    c = jnp.concatenate([jnp.ones((n, MLA_NOPE), F32), cos, cos, z(32)], axis=1)
    sa = jnp.concatenate([z(MLA_NOPE), -sin, z(16), z(32)], axis=1)
    sb = jnp.concatenate([z(MLA_NOPE), z(16), sin, z(32)], axis=1)
    return c, sa, sb


def _swa_rope_lanes(cos, sin):
    z = jnp.zeros_like(sin)
    return (jnp.concatenate([cos] * 4, axis=1), jnp.concatenate([-sin, z, -sin, z], axis=1),
            jnp.concatenate([z, sin, z, sin], axis=1))


def _identity_rope_lanes(n):
    lane = jnp.arange(LANES)[None, :]
    zero = jnp.zeros((n, LANES), F32)
    return ((jnp.where(lane < MLA_QK, 1.0, 0.0) + zero, zero, zero), (zero + 1.0, zero, zero))


def _mod_kernel(c_ref, w_ref, b_ref, o_ref):
    s = _silu(c_ref[...])
    o_ref[...] = jnp.dot(s, w_ref[...], precision=HI, preferred_element_type=F32) + b_ref[...]


def _mod_call(cc, mod_w, mod_b):
    rows, n = cc.shape[0], mod_w.shape[1]
    tn = 512
    return pl.pallas_call(
        _mod_kernel,
        grid=(n // tn,),
        in_specs=[pl.BlockSpec((rows, D_MODEL), lambda j: (0, 0)),
                  pl.BlockSpec((D_MODEL, tn), lambda j: (0, j)),
                  pl.BlockSpec((1, tn), lambda j: (0, j))],
        out_specs=pl.BlockSpec((rows, tn), lambda j: (0, j)),
        out_shape=jax.ShapeDtypeStruct((rows, n), F32),
        name="mod",
    )(cc, mod_w, mod_b.reshape(1, n))


def _rope(x, tabs, sh):
    c, sa, sb = tabs
    return x * c + pltpu.roll(x, LANES - sh, 1) * sa + pltpu.roll(x, sh, 1) * sb


def _proj_kernel(x_ref, shift_ref, scale_ref, g_ref, w_ref, gq_ref, wuq_ref, gkv_ref, wukv_ref,
                 cm_ref, sam_ref, sbm_ref, cs_ref, sas_ref, sbs_ref,
                 qm_ref, km_ref, vm_ref, qs_ref, ks_ref, vs_ref, gate_ref, hu_ref):
    x = x_ref[0]
    y = _rms(x, g_ref[...])
    h = (y * (1.0 + scale_ref[0]) + shift_ref[0]).astype(BF16)

    def seg(first, last=None):
        a, b = SEG[first][0], SEG[last or first][1]
        return jnp.dot(h, w_ref[:, a:b], preferred_element_type=F32)

    tm_ = (cm_ref[...], sam_ref[...], sbm_ref[...])
    ts_ = (cs_ref[...], sas_ref[...], sbs_ref[...])
    log2e = math.log2(math.e)

    qn = _rms(seg("mq"), gq_ref[...]).astype(BF16)
    q2 = jnp.dot(qn, wuq_ref[...], preferred_element_type=F32)
    for hd in range(MLA_HEADS):
        blk = slice(LANES * hd, LANES * (hd + 1))
        qm_ref[0, :, blk] = (_rope(q2[:, blk], tm_, MLA_ROPE // 2) * (MLA_QK ** -0.5 * log2e)).astype(BF16)

    pkv = seg("mkv", "kr")
    kvn = _rms(pkv[:, :MLA_KV_RANK], gkv_ref[...]).astype(BF16)
    kv = jnp.dot(kvn, wukv_ref[...], preferred_element_type=F32)
    kr = _rope(pkv[:, MLA_KV_RANK:], tm_, MLA_ROPE // 2)
    half = MLA_HEADS * LANES
    for hd in range(MLA_HEADS):
        blk = slice(LANES * hd, LANES * (hd + 1))
        km_ref[0, :, blk] = (kv[:, blk] + kr).astype(BF16)
    vm_ref[0] = kv[:, half:].T.astype(BF16)

    sq = seg("sq")
    for j in range(3):
        blk = slice(LANES * j, LANES * (j + 1))
        qs_ref[0, :, blk] = (_rope(sq[:, blk], ts_, SWA_DIM // 2) * (SWA_DIM ** -0.5 * log2e)).astype(BF16)
    skv = seg("sk", "sv")
    ks_ref[0] = _rope(skv[:, :LANES], ts_, SWA_DIM // 2).astype(BF16)
    vs_ref[0] = skv[:, LANES:].T.astype(BF16)

    gate_ref[0, :, 0:384] = _silu(seg("mg")).astype(BF16)
    gate_ref[0, :, 384:768] = _silu(seg("sg")).astype(BF16)
    gate_ref[0, :, 768:1024] = _silu(seg("hg")).astype(BF16)
    hu_ref[0] = seg("hu").astype(BF16)


def _proj_call(x, shift, scale, norm_g, w_aug, gq, wuq, gkv, wukv, rope_m, rope_s, tm):
    nb, n, _ = x.shape
    row = lambda b, i: (b, i, 0)
    col = lambda b, i: (b, 0, i)
    per_b = lambda b, i: (b, 0, 0)
    const = lambda b, i: (0, 0)
    tab = lambda b, i: (i, 0)
    kvw = SWA_KV_HEADS * SWA_DIM
    outs = [((n, 768), (tm, 768), row, BF16), ((n, 768), (tm, 768), row, BF16),
            ((MLA_WIDTH, n), (MLA_WIDTH, tm), col, BF16),
            ((n, SWA_WIDTH), (tm, SWA_WIDTH), row, BF16), ((n, kvw), (tm, kvw), row, BF16),
            ((kvw, n), (kvw, tm), col, BF16),
            ((n, D_MODEL), (tm, D_MODEL), row, BF16), ((n, 3 * HY_WIDTH), (tm, 3 * HY_WIDTH), row, BF16)]
    return pl.pallas_call(
        _proj_kernel,
        grid=(nb, n // tm),
        in_specs=[pl.BlockSpec((1, tm, D_MODEL), row),
                  pl.BlockSpec((1, 1, D_MODEL), per_b), pl.BlockSpec((1, 1, D_MODEL), per_b),
                  pl.BlockSpec((1, D_MODEL), const),
                  pl.BlockSpec((D_MODEL, AUG_COLS), const),
                  pl.BlockSpec((1, MLA_Q_RANK), const), pl.BlockSpec(wuq.shape, const),
                  pl.BlockSpec((1, MLA_KV_RANK), const), pl.BlockSpec(wukv.shape, const)]
                 + [pl.BlockSpec((tm, LANES), tab)] * 6,
        out_specs=[pl.BlockSpec((1,) + blk, imap) for _, blk, imap, _ in outs],
        out_shape=[jax.ShapeDtypeStruct((nb,) + shp, dt) for shp, _, _, dt in outs],
        compiler_params=pltpu.CompilerParams(dimension_semantics=("parallel", "parallel")),
        name="proj",
    )(x, shift, scale, norm_g, w_aug, gq, wuq, gkv, wukv, *rope_m, *rope_s)


def _sublane_all(v, op):
    for sh in (4, 2, 1):
        v = op(v, pltpu.roll(v, sh, 0))
    return v


def _mla_kernel(*refs, n_lat, tk, has_lat, unroll):
    if has_lat:
        q_ref, kl_ref, vtl_ref, kc_ref, vtc_ref, o_ref, s_sc, mx_sc, m_sc, l_sc, acc_sc = refs
    else:
        q_ref, kc_ref, vtc_ref, o_ref, s_sc, mx_sc, m_sc, l_sc, acc_sc = refs
    tq = q_ref.shape[1]
    nt = (((1,), (1,)), ((), ()))
    heads = range(2)
    m_sc[...] = jnp.full(m_sc.shape, -1e30, F32)
    l_sc[...] = jnp.zeros(l_sc.shape, F32)
    acc_sc[...] = jnp.zeros(acc_sc.shape, F32)

    def scores(slot, hh, k):
        n = k.shape[0]
        q = q_ref[0, :, LANES * hh:LANES * (hh + 1)]
        s = lax.dot_general(k, q, nt, preferred_element_type=F32)
        s_sc[slot, hh, 0:n, :] = s
        mx_sc[slot, hh] = jnp.max(s.reshape(n // 8, 8, tq), axis=0)

    def update(slot, hh, vt):
        n = vt.shape[1]
        m_prev = m_sc[hh]
        m_new = jnp.maximum(m_prev, _sublane_all(mx_sc[slot, hh], jnp.maximum))
        alpha = jnp.exp2(m_prev - m_new)
        p3 = jnp.exp2(s_sc[slot, hh, 0:n, :].reshape(n // 8, 8, tq) - m_new[None])
        l_sc[hh] = alpha * l_sc[hh] + jnp.sum(p3, axis=0)
        pv = jnp.dot(vt, p3.reshape(n, tq).astype(BF16), preferred_element_type=F32)
        acc_sc[hh] = acc_sc[hh] * jnp.tile(alpha, (LANES // 8, 1)) + pv
        m_sc[hh] = m_new

    def k_lat(i, hh):
        return kl_ref[0, pl.ds(pl.multiple_of(i * tk, tk), tk), LANES * hh:LANES * (hh + 1)]

    def vt_lat(i):
        return vtl_ref[0, :, pl.ds(pl.multiple_of(i * tk, tk), tk)]

    n = n_lat // tk if has_lat else 0
    k_ctx = lambda hh: kc_ref[0, :, LANES * hh:LANES * (hh + 1)]
    if n == 0:
        for hh in heads:
            scores(0, hh, k_ctx(hh))
    else:
        for hh in heads:
            scores(0, hh, k_lat(0, hh))
        first = ((n - 1) // unroll) * unroll if unroll % 2 == 0 else 0
        if first:
            def body(j, carry):
                for t in range(unroll):
                    i = j * unroll + t
                    for hh in heads:
                        scores((t + 1) % 2, hh, k_lat(i + 1, hh))
                    for hh in heads:
                        update(t % 2, hh, vt_lat(i))
                return carry
            lax.fori_loop(0, first // unroll, body, 0)
        for i in range(first, n):
            for hh in heads:
                if i + 1 < n:
                    scores((i + 1) % 2, hh, k_lat(i + 1, hh))
                else:
                    scores((i + 1) % 2, hh, k_ctx(hh))
            for hh in heads:
                update(i % 2, hh, vt_lat(i))
    for hh in heads:
        update(n % 2, hh, vtc_ref[0])
    outs = []
    for hh in range(2):
        inv = 1.0 / _sublane_all(l_sc[hh], jnp.add)
        outs.append(acc_sc[hh] * jnp.tile(inv, (LANES // 8, 1)))
    row = lax.broadcasted_iota(jnp.int32, (LANES, tq), 0)
    o_ref[0] = jnp.where(row < MLA_V, outs[0], outs[1]).T.astype(o_ref.dtype)


def _mla_call(q, k_lat, vt_lat, k_ctx, vt_ctx, tq, tk, unroll=2):
    nb, nq, _ = q.shape
    nc = k_ctx.shape[1]
    has_lat = k_lat is not None
    qmap = lambda b, p, i: (b, i, p)
    kvmap = lambda b, p, i: (b, 0, p)
    vtmap = lambda b, p, i: (b, p, 0)
    in_specs = [pl.BlockSpec((1, tq, 2 * LANES), qmap)]
    args = [q]
    n_lat = 0
    if has_lat:
        n_lat = k_lat.shape[1]
        in_specs += [pl.BlockSpec((1, n_lat, 2 * LANES), kvmap), pl.BlockSpec((1, LANES, n_lat), vtmap)]
        args += [k_lat, vt_lat]
    in_specs += [pl.BlockSpec((1, nc, 2 * LANES), kvmap), pl.BlockSpec((1, LANES, nc), vtmap)]
    args += [k_ctx, vt_ctx]
    return pl.pallas_call(
        functools.partial(_mla_kernel, n_lat=n_lat, tk=tk, has_lat=has_lat, unroll=unroll),
        grid=(nb, MLA_HEADS // 2, nq // tq),
        in_specs=in_specs,
        out_specs=pl.BlockSpec((1, tq, LANES), qmap),
        out_shape=jax.ShapeDtypeStruct((nb, nq, MLA_WIDTH), BF16),
        scratch_shapes=[pltpu.VMEM((2, 2, max(tk, nc), tq), F32), pltpu.VMEM((2, 2, 8, tq), F32),
                        pltpu.VMEM((2, 8, tq), F32), pltpu.VMEM((2, 8, tq), F32), pltpu.VMEM((2, LANES, tq), F32)],
        compiler_params=pltpu.CompilerParams(dimension_semantics=("parallel", "parallel", "parallel")),
        name="mla_lat" if has_lat else "mla_ctx",
    )(*args)


def _swa_kernel(*refs, seq, band):
    if band:
        sink_ref, q_ref, kp_ref, kc_ref, kn_ref, vp_ref, vc_ref, vn_ref, kx_ref, vx_ref, o_ref, s_sc, mx_sc = refs
    else:
        sink_ref, q_ref, kx_ref, vx_ref, o_ref, s_sc, mx_sc = refs
    tq = q_ref.shape[1]
    nt = (((1,), (1,)), ((), ()))
    lane = lax.broadcasted_iota(jnp.int32, (tq, LANES), 1)
    row = lax.broadcasted_iota(jnp.int32, (LANES, tq), 0)
    keys, vals = kx_ref[0], vx_ref[0]
    nk = keys.shape[0]
    if band:
        i = pl.program_id(1)
        keys = jnp.concatenate([keys, kp_ref[0], kc_ref[0], kn_ref[0]], axis=0)
        vals = jnp.concatenate([vals, vp_ref[0], vc_ref[0], vn_ref[0]], axis=1)
        nb = tq + 2 * SWA_WINDOW
        kpos = i * tq - SWA_WINDOW + lax.broadcasted_iota(jnp.int32, (nb, tq), 0)
        qpos = i * tq + lax.broadcasted_iota(jnp.int32, (nb, tq), 1)
        valid = (jnp.abs(kpos - qpos) <= SWA_WINDOW) & (kpos >= 0) & (kpos < seq)

    for j in range(SWA_HEADS // 2):
        qp = q_ref[0, :, LANES * j:LANES * (j + 1)]
        for g in range(SWA_KV_HEADS):
            q = jnp.where((lane >= SWA_DIM * g) & (lane < SWA_DIM * (g + 1)), qp, jnp.zeros_like(qp))
            s = lax.dot_general(keys, q, nt, preferred_element_type=F32)
            if band:
                s = jnp.concatenate([s[:nk], jnp.where(valid, s[nk:], -1e30)], axis=0)
            s_sc[2 * j + g] = s
            mx_sc[2 * j + g] = jnp.max(s.reshape(-1, 8, tq), axis=0)
    for j in range(SWA_HEADS // 2):
        res = []
        for g in range(SWA_KV_HEADS):
            sink = sink_ref[SWA_ORDER[2 * j + g]] * math.log2(math.e)
            m = jnp.maximum(_sublane_all(mx_sc[2 * j + g], jnp.maximum), sink)
            p3 = jnp.exp2(s_sc[2 * j + g].reshape(-1, 8, tq) - m[None])
            denom = _sublane_all(jnp.sum(p3, axis=0), jnp.add) + jnp.exp2(sink - m)
            o = jnp.dot(vals, p3.reshape(-1, tq).astype(BF16), preferred_element_type=F32)
            res.append(o * jnp.tile(1.0 / denom, (LANES // 8, 1)))
        o_ref[0, :, LANES * j:LANES * (j + 1)] = jnp.where(row < SWA_DIM, res[0], res[1]).T.astype(o_ref.dtype)


def _swa_call(sink, q, k, vt, k_ctx, vt_ctx, band):
    nb, nq, _ = q.shape
    nc = k_ctx.shape[1]
    tq = min(nq, 256) if band else nq
    nblk = nq // tq
    nkeys = nc + (tq + 2 * SWA_WINDOW if band else 0)
    wb = tq // SWA_WINDOW
    nwb = nq // SWA_WINDOW
    kw = SWA_KV_HEADS * SWA_DIM
    cur = lambda b, i: (b, i, 0)
    ctx = lambda b, i: (b, 0, 0)
    in_specs = [pl.BlockSpec(memory_space=pltpu.SMEM), pl.BlockSpec((1, tq, SWA_WIDTH), cur)]
    args = [sink, q]
    if band:
        in_specs += [pl.BlockSpec((1, SWA_WINDOW, kw), lambda b, i: (b, jnp.maximum(i * wb - 1, 0), 0)),
                     pl.BlockSpec((1, tq, kw), cur),
                     pl.BlockSpec((1, SWA_WINDOW, kw), lambda b, i: (b, jnp.minimum((i + 1) * wb, nwb - 1), 0)),
                     pl.BlockSpec((1, kw, SWA_WINDOW), lambda b, i: (b, 0, jnp.maximum(i * wb - 1, 0))),
                     pl.BlockSpec((1, kw, tq), lambda b, i: (b, 0, i)),
                     pl.BlockSpec((1, kw, SWA_WINDOW), lambda b, i: (b, 0, jnp.minimum((i + 1) * wb, nwb - 1)))]
        args += [k, k, k, vt, vt, vt]
    in_specs += [pl.BlockSpec((1, nc, kw), ctx), pl.BlockSpec((1, kw, nc), ctx)]
    args += [k_ctx, vt_ctx]
    return pl.pallas_call(
        functools.partial(_swa_kernel, seq=nq, band=band),
        grid=(nb, nblk),
        in_specs=in_specs,
        out_specs=pl.BlockSpec((1, tq, SWA_WIDTH), cur),
        out_shape=jax.ShapeDtypeStruct((nb, nq, SWA_WIDTH), BF16),
        scratch_shapes=[pltpu.VMEM((SWA_HEADS, nkeys, tq), F32), pltpu.VMEM((SWA_HEADS, 8, tq), F32)],
        compiler_params=pltpu.CompilerParams(dimension_semantics=("parallel", "parallel")),
        name="swa_lat" if band else "swa_ctx",
    )(*args)


def _filter_kernel(z_ref, w1_ref, b1_ref, fr_ref, w2_ref, b2_ref, w3_ref, b3_ref, dl_ref,
                   hf_ref, hb_ref, nrm_ref, *, num_tokens):
    i = pl.program_id(0)
    tl = z_ref.shape[0]
    fr = fr_ref[...]
    h = jnp.sin(fr * (jnp.dot(z_ref[...], w1_ref[...], precision=HI, preferred_element_type=F32) + b1_ref[...]))
    h = jnp.sin(fr * (jnp.dot(h, w2_ref[...], precision=HI, preferred_element_type=F32) + b2_ref[...]))
    h = jnp.dot(h, w3_ref[...], precision=HI, preferred_element_type=F32) + b3_ref[...]
    row = i * tl + lax.broadcasted_iota(jnp.int32, (tl, HY_WIDTH), 0)
    t = row.astype(F32) * (1.0 / (num_tokens - 1))
    decay = jnp.exp(-t * dl_ref[...])
    hf = h[:, :HY_WIDTH] * decay
    hb = jnp.where(row == 0, 0.0, h[:, HY_WIDTH:] * decay)
    hf_ref[...] = hf.astype(hf_ref.dtype)
    hb_ref[...] = hb.astype(hb_ref.dtype)
    part = jnp.sum(jnp.abs(hf) + jnp.abs(hb), axis=0, keepdims=True)

    @pl.when(i == 0)
    def _():
        nrm_ref[...] = jnp.zeros(nrm_ref.shape, F32)

    nrm_ref[...] += part


def _filter_call(lp, num_tokens, out_dtype):
    tl = min(num_tokens, 1024)
    t = jnp.linspace(0.0, 1.0, num_tokens, dtype=F32)[:, None]
    w = (2.0 * math.pi / num_tokens) * jnp.arange(num_tokens, dtype=F32)[:, None]
    bands = jnp.linspace(1e-4, HY_BANDS - 1, HY_BANDS, dtype=F32)[None, :]
    z = jnp.concatenate([t, jnp.cos(bands * w), -jnp.sin(bands * w),
                         jnp.zeros((num_tokens, LANES - HY_EMB), F32)], axis=-1)
    w1 = jnp.concatenate([lp["hy_w1"], jnp.zeros((LANES - HY_EMB, HY_HIDDEN), F32)], axis=0)
    deltas = jnp.abs(jnp.linspace(HY_DECAY_FAST, HY_DECAY_SLOW, HY_WIDTH, dtype=F32)).reshape(1, HY_WIDTH)
    const = lambda i: (0, 0)
    rowm = lambda i: (i, 0)
    r1 = lambda a: a.reshape(1, -1)
    return pl.pallas_call(
        functools.partial(_filter_kernel, num_tokens=num_tokens),
        grid=(num_tokens // tl,),
        in_specs=[pl.BlockSpec((tl, LANES), rowm),
                  pl.BlockSpec((LANES, HY_HIDDEN), const), pl.BlockSpec((1, HY_HIDDEN), const),
                  pl.BlockSpec((1, HY_HIDDEN), const),
                  pl.BlockSpec((HY_HIDDEN, HY_HIDDEN), const), pl.BlockSpec((1, HY_HIDDEN), const),
                  pl.BlockSpec((HY_HIDDEN, 2 * HY_WIDTH), const), pl.BlockSpec((1, 2 * HY_WIDTH), const),
                  pl.BlockSpec((1, HY_WIDTH), const)],
        out_specs=[pl.BlockSpec((tl, HY_WIDTH), rowm), pl.BlockSpec((tl, HY_WIDTH), rowm),
                   pl.BlockSpec((1, HY_WIDTH), const)],
        out_shape=[jax.ShapeDtypeStruct((num_tokens, HY_WIDTH), out_dtype)] * 2
                  + [jax.ShapeDtypeStruct((1, HY_WIDTH), F32)],
        compiler_params=pltpu.CompilerParams(dimension_semantics=("arbitrary",)),
        name="hy_filter",
    )(z, w1, r1(lp["hy_b1"]), r1(lp["hy_freq"]), lp["hy_w2"], r1(lp["hy_b2"]), lp["hy_w3"], r1(lp["hy_b3"]), deltas)


def _short_conv(u, prev_row, next_row, w_ref, b_ref):
    n = u.shape[0]
    row = lax.broadcasted_iota(jnp.int32, u.shape, 0)
    up = jnp.where(row == 0, prev_row, pltpu.roll(u, 1, 0))
    un = jnp.where(row == n - 1, next_row, pltpu.roll(u, n - 1, 0))
    return up * w_ref[0:1, :] + u * w_ref[1:2, :] + un * w_ref[2:3, :] + b_ref[...]


HALO = 16


def _hconv_kernel(u_ref, up_ref, un_ref, w_ref, b_ref, x0_ref, z_ref, zb_ref):
    i = pl.program_id(1)
    last = pl.num_programs(1) - 1
    prev_row = jnp.where(i == 0, 0.0, up_ref[0, HALO - 1:HALO, :].astype(F32))
    next_row = jnp.where(i == last, 0.0, un_ref[0, 0:1, :].astype(F32))
    uc = _short_conv(u_ref[0].astype(F32), prev_row, next_row, w_ref, b_ref)
    x0_ref[0] = uc[:, :HY_WIDTH]
    z = uc[:, 2 * HY_WIDTH:] * uc[:, HY_WIDTH:2 * HY_WIDTH]
    z_ref[0] = z
    zb_ref[0] = z.astype(BF16)


def _hconv_call(hu, conv_w, conv_b, tl):
    nb, n, w3 = hu.shape
    hb = tl // HALO
    nh = n // HALO
    cur = lambda b, i: (b, i, 0)
    prev = lambda b, i: (b, jnp.maximum(i * hb - 1, 0), 0)
    nxt = lambda b, i: (b, jnp.minimum((i + 1) * hb, nh - 1), 0)
    const = lambda b, i: (0, 0)
    return pl.pallas_call(
        _hconv_kernel,
        grid=(nb, n // tl),
        in_specs=[pl.BlockSpec((1, tl, w3), cur), pl.BlockSpec((1, HALO, w3), prev), pl.BlockSpec((1, HALO, w3), nxt),
                  pl.BlockSpec((3, w3), const), pl.BlockSpec((1, w3), const)],
        out_specs=[pl.BlockSpec((1, tl, HY_WIDTH), cur)] * 3,
        out_shape=[jax.ShapeDtypeStruct((nb, n, HY_WIDTH), d) for d in (F32, F32, BF16)],
        compiler_params=pltpu.CompilerParams(dimension_semantics=("parallel", "parallel")),
        name="hy_conv",
    )(hu, hu, hu, conv_w, conv_b.reshape(1, w3))


def _dft_tables(seq):
    n1 = 2 * seq // FFT_N2
    nd = seq // FFT_N2
    n = n1 * FFT_N2
    k1 = np.arange(n1)[:, None]
    a = 2.0 * np.pi * ((k1 * np.arange(nd)[None, :]) % n1) / n1
    c, s = np.cos(a), np.sin(a)
    fwd1 = np.block([[c, s], [-s, c]])
    inv1 = np.block([[c.T, -s.T], [s.T, c.T]])
    k2 = np.arange(FFT_N2)[None, :, None]
    n2 = np.arange(FFT_N2)[None, None, :]
    g = 2.0 * np.pi * ((n2 * (k1[:, :, None] + n1 * k2)) % n) / n
    gr, gi = np.cos(g), -np.sin(g)
    cast = lambda t: jnp.asarray(np.ascontiguousarray(t, dtype=np.float32), BF16)
    return (cast(fwd1), cast(inv1), cast(gr), cast(gi),
            cast(np.transpose(gr, (0, 2, 1))), cast(np.transpose(-gi, (0, 2, 1))))


def _stage1_kernel(m_ref, z_ref, a_ref):
    zs = jnp.concatenate([z_ref[0, j] for j in range(z_ref.shape[1])], axis=0).astype(BF16)
    a_ref[0] = jnp.dot(m_ref[...], zs, preferred_element_type=F32).astype(a_ref.dtype)


def _stage1_call(mat, z, out_dtype, tn, name):
    npair, parts, rows, cols = z.shape
    mo = mat.shape[0]
    return pl.pallas_call(
        _stage1_kernel,
        grid=(npair, cols // tn),
        in_specs=[pl.BlockSpec(mat.shape, lambda p, j: (0, 0)),
                  pl.BlockSpec((1, parts, rows, tn), lambda p, j: (p, 0, 0, j))],
        out_specs=pl.BlockSpec((1, mo, tn), lambda p, j: (p, 0, j)),
        out_shape=jax.ShapeDtypeStruct((npair, mo, cols), out_dtype),
        compiler_params=pltpu.CompilerParams(dimension_semantics=("parallel", "parallel")),
        name=name,
    )(mat, z)


def _cplx_mat(re, im):
    return jnp.concatenate([jnp.concatenate([re, -im], axis=1), jnp.concatenate([im, re], axis=1)], axis=0)


def _spec_kernel(gr_ref, gi_ref, a_ref, nrm_ref, k_ref, *, kb, scale):
    inv = scale / (nrm_ref[...] + NORM_EPS)
    for j in range(kb):
        g2 = _cplx_mat(gr_ref[j], gi_ref[j])
        x1 = jnp.dot(g2, jnp.concatenate([a_ref[0, 0, j], a_ref[0, 1, j]], axis=0), preferred_element_type=F32)
        x2 = jnp.dot(g2, jnp.concatenate([a_ref[1, 0, j], a_ref[1, 1, j]], axis=0), preferred_element_type=F32)
        k_ref[j, :FFT_N2, :] = (x1[:FFT_N2] + x2[:FFT_N2]) * inv
        k_ref[j, FFT_N2:, :] = (x1[FFT_N2:] - x2[FFT_N2:]) * inv


def _spec_call(gr, gi, a, nrm, kb, scale):
    n1 = gr.shape[0]
    tab = lambda i: (i, 0, 0)
    return pl.pallas_call(
        functools.partial(_spec_kernel, kb=kb, scale=scale),
        grid=(n1 // kb,),
        in_specs=[pl.BlockSpec((kb, FFT_N2, FFT_N2), tab), pl.BlockSpec((kb, FFT_N2, FFT_N2), tab),
                  pl.BlockSpec((2, 2, kb, FFT_N2, HY_WIDTH), lambda i: (0, 0, i, 0, 0)),
                  pl.BlockSpec((1, HY_WIDTH), lambda i: (0, 0))],
        out_specs=pl.BlockSpec((kb, 2 * FFT_N2, HY_WIDTH), tab),
        out_shape=jax.ShapeDtypeStruct((n1, 2 * FFT_N2, HY_WIDTH), F32),
        compiler_params=pltpu.CompilerParams(dimension_semantics=("parallel",)),
        name="hy_spec",
    )(gr, gi, a, nrm)


def _stage2_kernel(gr_ref, gi_ref, hr_ref, hi_ref, k_ref, a_ref, b_ref, *, kb):
    for j in range(kb):
        g2 = _cplx_mat(gr_ref[j], gi_ref[j])
        h2 = _cplx_mat(hr_ref[j], hi_ref[j])
        kr, ki = k_ref[j, :FFT_N2, :], k_ref[j, FFT_N2:, :]
        for p in range(a_ref.shape[0]):
            s = jnp.concatenate([a_ref[p, 0, j], a_ref[p, 1, j]], axis=0)
            x = jnp.dot(g2, s, preferred_element_type=F32)
            xr, xi = x[:FFT_N2], x[FFT_N2:]
            y = jnp.concatenate([xr * kr - xi * ki, xr * ki + xi * kr], axis=0).astype(BF16)
            b = jnp.dot(h2, y, preferred_element_type=F32)
            b_ref[p, 0, j] = b[:FFT_N2].astype(b_ref.dtype)
            b_ref[p, 1, j] = b[FFT_N2:].astype(b_ref.dtype)


def _stage2_call(gr, gi, hr, hi, kf, a, kb):
    npair = a.shape[0]
    n1 = gr.shape[0]
    tab = lambda i: (i, 0, 0)
    dat = lambda i: (0, 0, i, 0, 0)
    return pl.pallas_call(
        functools.partial(_stage2_kernel, kb=kb),
        grid=(n1 // kb,),
        in_specs=[pl.BlockSpec((kb, FFT_N2, FFT_N2), tab)] * 4
                 + [pl.BlockSpec((kb, 2 * FFT_N2, HY_WIDTH), tab),
                    pl.BlockSpec((npair, 2, kb, FFT_N2, HY_WIDTH), dat)],
        out_specs=pl.BlockSpec((npair, 2, kb, FFT_N2, HY_WIDTH), dat),
        out_shape=jax.ShapeDtypeStruct(a.shape, BF16),
        compiler_params=pltpu.CompilerParams(dimension_semantics=("parallel",)),
        name="hy_stage2",
    )(gr, gi, hr, hi, kf, a)


def _long_conv(z, hf, hb, nrm, tables):
    nb, seq, w = z.shape
    fwd1, inv1, gr, gi, hr, hi = tables
    n1 = 2 * seq // FFT_N2
    nd = seq // FFT_N2
    cols = FFT_N2 * w
    tn = min(cols, 4096)
    kb = min(n1, 8)
    zf = jnp.stack([hf, hb]).reshape(2, 1, nd, cols)
    af = _stage1_call(fwd1[:, :nd], zf, BF16, tn, "hy_fstage1").reshape(2, 2, n1, FFT_N2, w)
    kf = _spec_call(gr, gi, af, nrm, kb, 1.0 / (n1 * FFT_N2))
    zd = z.reshape(nb // 2, 2, nd, cols)
    a = _stage1_call(fwd1, zd, BF16, tn, "hy_stage1").reshape(nb // 2, 2, n1, FFT_N2, w)
    b = _stage2_call(gr, gi, hr, hi, kf, a, kb).reshape(nb // 2, 2, n1, cols)
    y = _stage1_call(inv1, b, BF16, tn, "hy_istage1")
    return y.reshape(nb, seq, w)


def _hctx_kernel(u_ref, w_ref, b_ref, hf_ref, hb_ref, nrm_ref, c_ref, s_ref, ct_ref, st_ref,
                 x0_ref, z_ref, y_ref):
    u = u_ref[0].astype(F32)
    n = u.shape[0]
    zero_row = jnp.zeros((1, u.shape[1]), F32)
    uc = _short_conv(u, zero_row, zero_row, w_ref, b_ref)
    x0 = uc[:, :HY_WIDTH]
    z = uc[:, 2 * HY_WIDTH:] * uc[:, HY_WIDTH:2 * HY_WIDTH]
    dot = lambda a, b: jnp.dot(a, b, precision=HI, preferred_element_type=F32)
    c, s = c_ref[...], s_ref[...]
    inv = (1.0 / (2 * n)) / (nrm_ref[...] + NORM_EPS)
    kr = (dot(c, hf_ref[...]) + dot(c, hb_ref[...])) * inv
    ki = (dot(s, hb_ref[...]) - dot(s, hf_ref[...])) * inv
    xr, xi = dot(c, z), -dot(s, z)
    yr, yi = xr * kr - xi * ki, xr * ki + xi * kr
    x0_ref[0] = x0
    z_ref[0] = z
    y_ref[0] = dot(ct_ref[...], yr) - dot(st_ref[...], yi)


def _hctx_call(hu, conv_w, conv_b, hf, hb, nrm):
    nb, n, w3 = hu.shape
    k = np.arange(2 * n)[:, None]
    a = 2.0 * np.pi * ((k * np.arange(n)[None, :]) % (2 * n)) / (2 * n)
    c, s = np.cos(a).astype(np.float32), np.sin(a).astype(np.float32)
    const = lambda b: (0, 0)
    cur = lambda b: (b, 0, 0)
    out = pl.BlockSpec((1, n, HY_WIDTH), cur)
    return pl.pallas_call(
        _hctx_kernel,
        grid=(nb,),
        in_specs=[pl.BlockSpec((1, n, w3), cur), pl.BlockSpec((3, w3), const), pl.BlockSpec((1, w3), const),
                  pl.BlockSpec((n, HY_WIDTH), const), pl.BlockSpec((n, HY_WIDTH), const),
                  pl.BlockSpec((1, HY_WIDTH), const),
                  pl.BlockSpec((2 * n, n), const), pl.BlockSpec((2 * n, n), const),
                  pl.BlockSpec((n, 2 * n), const), pl.BlockSpec((n, 2 * n), const)],
        out_specs=[out, out, out],
        out_shape=[jax.ShapeDtypeStruct((nb, n, HY_WIDTH), F32)] * 3,
        compiler_params=pltpu.CompilerParams(dimension_semantics=("parallel",)),
        name="hy_ctx",
    )(hu, conv_w, conv_b.reshape(1, w3), hf, hb, nrm, jnp.asarray(c), jnp.asarray(s),
      jnp.asarray(c.T.copy()), jnp.asarray(s.T.copy()))


def _merge_kernel(x_ref, oa_ref, ob_ref, x0_ref, z_ref, yc_ref, gate_ref, gx_ref, bias_ref, w_ref, fg_ref,
                  o_ref, *, final):
    g = gate_ref[0].astype(F32)
    ya = oa_ref[0].astype(F32) * g[:, 0:384]
    yb = ob_ref[0].astype(F32) * g[:, 384:768]
    z = z_ref[0]
    yh = x0_ref[0] * (yc_ref[0].astype(F32) + bias_ref[...] * z) * g[:, 768:1024]
    y = jnp.concatenate([ya, yb, yh], axis=1).astype(BF16)
    xn = x_ref[0] + gx_ref[0] * jnp.dot(y, w_ref[...], preferred_element_type=F32)
    if final:
        xn = _rms(xn, fg_ref[...])
    o_ref[0] = xn


def _merge_call(x, oa, ob, x0, z, yc, gates, gate_x, bias, w_out, fg, tm, final):
    nb, n, _ = x.shape
    row = lambda b, i: (b, i, 0)
    per_b = lambda b, i: (b, 0, 0)
    const = lambda b, i: (0, 0)
    return pl.pallas_call(
        functools.partial(_merge_kernel, final=final),
        grid=(nb, n // tm),
        in_specs=[pl.BlockSpec((1, tm, D_MODEL), row),
                  pl.BlockSpec((1, tm, MLA_WIDTH), row), pl.BlockSpec((1, tm, SWA_WIDTH), row),
                  pl.BlockSpec((1, tm, HY_WIDTH), row), pl.BlockSpec((1, tm, HY_WIDTH), row),
                  pl.BlockSpec((1, tm, HY_WIDTH), row),
                  pl.BlockSpec((1, tm, D_MODEL), row), pl.BlockSpec((1, 1, D_MODEL), per_b),
                  pl.BlockSpec((1, HY_WIDTH), const), pl.BlockSpec((D_MIX, D_MODEL), const),
                  pl.BlockSpec((1, D_MODEL), const)],
        out_specs=pl.BlockSpec((1, tm, D_MODEL), row),
        out_shape=jax.ShapeDtypeStruct((nb, n, D_MODEL), F32),
        compiler_params=pltpu.CompilerParams(dimension_semantics=("parallel", "parallel")),
        name="merge",
    )(x, oa, ob, x0, z, yc, gates, gate_x, bias, w_out, fg)


def _layer(x, xc, mod, lp, consts, update_ctx, final, final_g):
    nb, seq, _ = x.shape
    nctx = xc.shape[1]
    w_aug = _prep_w_in(lp["w_in"])
    w_out = _prep_w_out(lp["w_out"])
    wuq = _prep_w_uq(lp["mla_w_uq"])
    wukv = _prep_w_ukv(lp["mla_w_ukv"])
    r1 = lambda a: a.reshape(1, -1)
    shift, scale, gate = (mod[:, D_MODEL * j:D_MODEL * (j + 1)] for j in range(3))
    sel = lambda a, lo, hi: a[lo:hi].reshape(hi - lo, 1, D_MODEL)
    ctx_b = lambda a: jnp.broadcast_to(a[nb:nb + 1].reshape(1, 1, D_MODEL), (nb, 1, D_MODEL))

    tm = min(seq, 512)
    px = _proj_call(x, sel(shift, 0, nb), sel(scale, 0, nb), r1(lp["norm_g"]), w_aug, r1(lp["mla_q_norm"]), wuq,
                    r1(lp["mla_kv_norm"]), wukv, *consts["rope_lat"], tm)
    pc = _proj_call(xc, ctx_b(shift), ctx_b(scale), r1(lp["norm_g"]), w_aug, r1(lp["mla_q_norm"]), wuq,
                    r1(lp["mla_kv_norm"]), wukv, *consts["rope_ctx"], nctx)
    qm, km, vm, qs, ks, vs, gates, hu = px
    qm_c, km_c, vm_c, qs_c, ks_c, vs_c, gates_c, hu_c = pc

    o_a = _mla_call(qm, km, vm, km_c, vm_c, min(seq, 512), 1024 if seq >= 2048 else 128, 2)
    o_b = _swa_call(lp["swa_sink"], qs, ks, vs, ks_c, vs_c, True)

    hf, hb, nrm = _filter_call(lp, seq, BF16)
    x0, z, zb = _hconv_call(hu, lp["hy_conv_w"], lp["hy_conv_b"], min(seq, 1024))
    yc = _long_conv(zb, hf, hb, nrm, consts["dft"])

    x_new = _merge_call(x, o_a, o_b, x0, z, yc, gates, sel(gate, 0, nb), r1(lp["hy_bias"]),
                        w_out, r1(final_g), tm, final)
    if update_ctx:
        oc_a = _mla_call(qm_c, None, None, km_c, vm_c, nctx, nctx)
        oc_b = _swa_call(lp["swa_sink"], qs_c, None, None, ks_c, vs_c, False)
        hf_c, hb_c, nrm_c = _filter_call(lp, nctx, F32)
        x0_c, z_c, yc_c = _hctx_call(hu_c, lp["hy_conv_w"], lp["hy_conv_b"], hf_c, hb_c, nrm_c)
        xc = _merge_call(xc, oc_a, oc_b, x0_c, z_c, yc_c, gates_c, ctx_b(gate), r1(lp["hy_bias"]),
                         w_out, r1(final_g), nctx, False)
    return x_new, xc


def _forward(x, c, ctx, c_ctx, params, final_norm_g):
    nb, seq, _ = x.shape
    nctx = ctx.shape[1]
    consts = {"rope_lat": (_mla_rope_lanes(*_rope_tables(seq, MLA_ROPE)), _swa_rope_lanes(*_rope_tables(seq, SWA_DIM))),
              "rope_ctx": _identity_rope_lanes(nctx), "dft": _dft_tables(seq)}
    cc = jnp.concatenate([c, c_ctx[None, :], jnp.zeros((8 - nb - 1, D_MODEL), F32)], axis=0)
    xc = ctx
    depth = params["w_in"].shape[0]
    for l in range(depth):
        lp = {k: v[l] for k, v in params.items()}
        mod = _mod_call(cc, lp["mod_w"], lp["mod_b"])
        x, xc = _layer(x, xc, mod, lp, consts, l < depth - 1, l == depth - 1, final_norm_g)
    return x


def kernel(x, c, ctx, c_ctx, norm_g, mod_w, mod_b, w_in, mla_q_norm, mla_w_uq, mla_kv_norm, mla_w_ukv, swa_sink,
           hy_conv_w, hy_conv_b, hy_w1, hy_b1, hy_freq, hy_w2, hy_b2, hy_w3, hy_b3, hy_bias, w_out, final_norm_g):
    params = dict(norm_g=norm_g, mod_w=mod_w, mod_b=mod_b, w_in=w_in, mla_q_norm=mla_q_norm, mla_w_uq=mla_w_uq,
                  mla_kv_norm=mla_kv_norm, mla_w_ukv=mla_w_ukv, swa_sink=swa_sink, hy_conv_w=hy_conv_w,
                  hy_conv_b=hy_conv_b, hy_w1=hy_w1, hy_b1=hy_b1, hy_freq=hy_freq, hy_w2=hy_w2, hy_b2=hy_b2,
                  hy_w3=hy_w3, hy_b3=hy_b3, hy_bias=hy_bias, w_out=w_out)
    return _forward(x, c, ctx, c_ctx, params, final_norm_g)
```
